```python
import jax, jax.numpy as jnp
from jax import lax
import numpy as np

D_MODEL = 2048
BATCH = 16
SEQ = 2048
DEPTH = 2

EPS = 1e-6
N_EVEN = (DEPTH + 1) // 2
N_ODD = DEPTH // 2
D_A = D_MODEL // 2
POOL_WINDOWS = (2, 4, 8, 16)
N_POOL_GROUPS = len(POOL_WINDOWS)
G_A = D_A // N_POOL_GROUPS
D_B = D_MODEL // 2
H_B = 4
DH_B = D_B // H_B
CHUNK = 128
D_IN_AB = D_A + 2 * D_B
D_MIX_AB = D_A + D_B
D_C = D_MODEL
CONV_K = 31
D_FF = ((8 * D_MODEL // 3 + 255) // 256) * 256

kernel_name = "hybrid_pool_sgu_conformer_block"


def rms_norm(x, g):
    xf = x.astype(jnp.float32)
    y = xf * lax.rsqrt(jnp.mean(xf * xf, axis=-1, keepdims=True) + EPS)
    return (y * g.astype(jnp.float32)).astype(x.dtype)


def layer_norm(x, g, b):
    xf = x.astype(jnp.float32)
    mu = jnp.mean(xf, axis=-1, keepdims=True)
    var = jnp.mean(jnp.square(xf - mu), axis=-1, keepdims=True)
    y = (xf - mu) * lax.rsqrt(var + EPS)
    return (y * g.astype(jnp.float32) + b.astype(jnp.float32)).astype(x.dtype)


def causal_window_mean(x, w):
    s = x.shape[1]
    cs = jnp.cumsum(x.astype(jnp.float32), axis=1)
    lag = jnp.pad(cs, ((0, 0), (w, 0), (0, 0)))[:, :s]
    count = jnp.minimum(jnp.arange(s) + 1, w).astype(jnp.float32)
    return ((cs - lag) / count[None, :, None]).astype(x.dtype)


def pool_mixer(a, w_group, scale):
    outs = []
    for g, w in enumerate(POOL_WINDOWS):
        ag = a[..., g * G_A:(g + 1) * G_A]
        d = causal_window_mean(ag, w) - ag
        outs.append(jnp.einsum('bsc,cd->bsd', d, w_group[g]))
    return jnp.concatenate(outs, axis=-1) * scale


def sgu_mixer(z, ln_g, ln_b, w_s, b_s):
    u, v = z[..., :D_B], z[..., D_B:]
    v = layer_norm(v, ln_g, ln_b)
    bsz, s, _ = v.shape
    v = v.reshape(bsz, s // CHUNK, CHUNK, H_B, DH_B)
    mask = jnp.tril(jnp.ones((CHUNK, CHUNK), dtype=bool))
    w = jnp.where(mask[None], w_s, jnp.zeros_like(w_s))
    mixed = jnp.einsum('hts,bcshd->bcthd', w, v) + b_s.T[None, None, :, :, None]
    return u * mixed.reshape(bsz, s, D_B)


def ab_layer(x, norm_pre, w_in, a_w_group, a_scale, b_ln_g, b_ln_b, b_w_s, b_b_s,
             w_out, norm_post):
    h = rms_norm(x, norm_pre)
    p = jnp.einsum('bsd,de->bse', h, w_in)
    a_out = pool_mixer(p[..., :D_A], a_w_group, a_scale)
    b_out = sgu_mixer(jax.nn.gelu(p[..., D_A:]), b_ln_g, b_ln_b, b_w_s, b_b_s)
    y = jnp.einsum('bse,ed->bsd', jnp.concatenate([a_out, b_out], axis=-1), w_out)
    return x + rms_norm(y, norm_post)


def conformer_conv_layer(x, norm_pre, w_pw1, b_pw1, w_dw, b_dw, ln_g, ln_b,
                         w_pw2, b_pw2, norm_post):
    h = rms_norm(x, norm_pre)
    p = jnp.einsum('bsd,de->bse', h, w_pw1) + b_pw1
    g = p[..., :D_C] * jax.nn.sigmoid(p[..., D_C:])
    conv = lax.conv_general_dilated(
        g, w_dw[:, None, :], window_strides=(1,), padding=[(CONV_K - 1, 0)],
        dimension_numbers=('NWC', 'WIO', 'NWC'), feature_group_count=D_C) + b_dw
    c = jax.nn.silu(layer_norm(conv, ln_g, ln_b))
    y = jnp.einsum('bse,ed->bsd', c, w_pw2) + b_pw2
    return x + rms_norm(y, norm_post)


def swiglu_ffn(x, norm_pre, w_gate, w_up, w_down, norm_post):
    h = rms_norm(x, norm_pre)
    a = jax.nn.silu(jnp.einsum('bsd,df->bsf', h, w_gate)) * jnp.einsum('bsd,df->bsf', h, w_up)
    y = jnp.einsum('bsf,fd->bsd', a, w_down)
    return x + rms_norm(y, norm_post)


def setup_inputs(seed: int = 0) -> dict:
    key = jax.random.key(seed)
    ks = jax.random.split(key, 32)
    f32 = jnp.float32
    nrm = lambda k, shape, scale: (jax.random.normal(k, shape, f32) * scale).astype(f32)
    gain = lambda k, shape: 1.0 + 0.05 * jax.random.normal(k, shape, f32)
    bias = lambda k, shape: 0.02 * jax.random.normal(k, shape, f32)
    NE, NO, L = N_EVEN, N_ODD, DEPTH
    return {
        "x": jax.random.normal(ks[0], (BATCH, SEQ, D_MODEL), f32),
        "ab_norm_pre": gain(ks[1], (NE, D_MODEL)),
        "ab_w_in": nrm(ks[2], (NE, D_MODEL, D_IN_AB), D_MODEL ** -0.5),
        "a_w_group": nrm(ks[3], (NE, N_POOL_GROUPS, G_A, G_A), G_A ** -0.5),
        "a_scale": 0.5 + 0.1 * jax.random.normal(ks[4], (NE, D_A), f32),
        "b_ln_g": gain(ks[5], (NE, D_B)),
        "b_ln_b": bias(ks[6], (NE, D_B)),
        "b_w_s": nrm(ks[7], (NE, H_B, CHUNK, CHUNK), CHUNK ** -0.5),
        "b_b_s": gain(ks[8], (NE, H_B, CHUNK)),
        "ab_w_out": nrm(ks[9], (NE, D_MIX_AB, D_MODEL), D_MIX_AB ** -0.5),
        "ab_norm_post": gain(ks[10], (NE, D_MODEL)),
        "c_norm_pre": gain(ks[11], (NO, D_MODEL)),
        "c_w_pw1": nrm(ks[12], (NO, D_MODEL, 2 * D_C), D_MODEL ** -0.5),
        "c_b_pw1": bias(ks[13], (NO, 2 * D_C)),
        "c_w_dw": nrm(ks[14], (NO, CONV_K, D_C), CONV_K ** -0.5),
        "c_b_dw": bias(ks[15], (NO, D_C)),
        "c_ln_g": gain(ks[16], (NO, D_C)),
        "c_ln_b": bias(ks[17], (NO, D_C)),
        "c_w_pw2": nrm(ks[18], (NO, D_C, D_MODEL), D_C ** -0.5),
        "c_b_pw2": bias(ks[19], (NO, D_MODEL)),
        "c_norm_post": gain(ks[20], (NO, D_MODEL)),
        "f_norm_pre": gain(ks[21], (L, D_MODEL)),
        "f_w_gate": nrm(ks[22], (L, D_MODEL, D_FF), D_MODEL ** -0.5),
        "f_w_up": nrm(ks[23], (L, D_MODEL, D_FF), D_MODEL ** -0.5),
        "f_w_down": nrm(ks[24], (L, D_FF, D_MODEL), D_FF ** -0.5),
        "f_norm_post": gain(ks[25], (L, D_MODEL)),
    }


def reference(x, ab_norm_pre, ab_w_in, a_w_group, a_scale, b_ln_g, b_ln_b, b_w_s, b_b_s,
              ab_w_out, ab_norm_post, c_norm_pre, c_w_pw1, c_b_pw1, c_w_dw, c_b_dw,
              c_ln_g, c_ln_b, c_w_pw2, c_b_pw2, c_norm_post, f_norm_pre, f_w_gate,
              f_w_up, f_w_down, f_norm_post):
    for layer in range(DEPTH):
        i = layer // 2
        if layer % 2 == 0:
            x = ab_layer(x, ab_norm_pre[i], ab_w_in[i], a_w_group[i], a_scale[i],
                         b_ln_g[i], b_ln_b[i], b_w_s[i], b_b_s[i], ab_w_out[i],
                         ab_norm_post[i])
        else:
            x = conformer_conv_layer(x, c_norm_pre[i], c_w_pw1[i], c_b_pw1[i], c_w_dw[i],
                                     c_b_dw[i], c_ln_g[i], c_ln_b[i], c_w_pw2[i],
                                     c_b_pw2[i], c_norm_post[i])
        x = swiglu_ffn(x, f_norm_pre[layer], f_w_gate[layer], f_w_up[layer],
                       f_w_down[layer], f_norm_post[layer])
    return x
```

```python
import functools

import jax
import jax.numpy as jnp
from jax import lax
from jax.experimental import pallas as pl
from jax.experimental.pallas import tpu as pltpu

F32 = jnp.float32
BF16 = jnp.bfloat16

EPS = 1e-6
POOL_WINDOWS = (2, 4, 8, 16)
SGU_HEADS = 4
SGU_CHUNK = 128
CONV_K = 31

V7X_VMEM_BYTES = 64 * 1024 * 1024
VMEM_LIMIT_BYTES = V7X_VMEM_BYTES - 8 * 1024 * 1024
SUBLANES = 8

POOL_HALO = 16
CONV_HALO = 32


def _rms_norm(xf, g):
    y = xf * lax.rsqrt(jnp.mean(xf * xf, axis=-1, keepdims=True) + EPS)
    return y * g


def _layer_norm(xf, g, b):
    mu = jnp.mean(xf, axis=-1, keepdims=True)
    var = jnp.mean(jnp.square(xf - mu), axis=-1, keepdims=True)
    return (xf - mu) * lax.rsqrt(var + EPS) * g + b


def _dot(a, b):
    return jnp.dot(a, b, preferred_element_type=F32)


def _carry_halo(scr_ref, halo, tile, first_tile):
    @pl.when(first_tile)
    def _():
        scr_ref[0:halo, :] = jnp.zeros((halo, scr_ref.shape[1]), scr_ref.dtype)

    @pl.when(jnp.logical_not(first_tile))
    def _():
        scr_ref[0:halo, :] = scr_ref[tile:tile + halo, :]


def _ab_body(x_ref, npre_ref, w_in_ref, wgrp_ref, ascale_ref, lng_ref, lnb_ref, ws_ref, bs_ref,
             w_out_ref, npost_ref, o_ref, a_scr, mix_scr, *, tile, d_a, d_b):
    i = pl.program_id(1)
    g_a = d_a // len(POOL_WINDOWS)
    dh_b = d_b // SGU_HEADS

    xf = x_ref[...]
    h = _rms_norm(xf, npre_ref[...]).astype(BF16)

    _carry_halo(a_scr, POOL_HALO, tile, i == 0)
    a_scr[POOL_HALO:POOL_HALO + tile, :] = _dot(h, w_in_ref[:, 0:d_a])
    pos = i * tile + lax.broadcasted_iota(jnp.int32, (tile, g_a), 0)
    for g, w in enumerate(POOL_WINDOWS):
        cols = slice(g * g_a, (g + 1) * g_a)
        ag = a_scr[POOL_HALO:POOL_HALO + tile, cols]
        s = ag
        for k in range(1, w):
            s = s + a_scr[POOL_HALO - k:POOL_HALO - k + tile, cols]
        count = jnp.minimum(pos + 1, w).astype(F32)
        d = s / count - ag
        og = _dot(d.astype(BF16), wgrp_ref[g])
        mix_scr[:, cols] = (og * ascale_ref[:, cols]).astype(BF16)

    z_u = jax.nn.gelu(_dot(h, w_in_ref[:, d_a:d_a + d_b]))
    z_v = jax.nn.gelu(_dot(h, w_in_ref[:, d_a + d_b:d_a + 2 * d_b]))
    v = _layer_norm(z_v, lng_ref[...], lnb_ref[...]).astype(BF16)
    row = lax.broadcasted_iota(jnp.int32, (SGU_CHUNK, SGU_CHUNK), 0)
    col = lax.broadcasted_iota(jnp.int32, (SGU_CHUNK, SGU_CHUNK), 1)
    for hd in range(SGU_HEADS):
        w_tril = jnp.where(col <= row, ws_ref[hd], 0.0).astype(BF16)
        bias = bs_ref[:, hd:hd + 1]
        hcols = slice(hd * dh_b, (hd + 1) * dh_b)
        for c in range(tile // SGU_CHUNK):
            rows = slice(c * SGU_CHUNK, (c + 1) * SGU_CHUNK)
            mixed = _dot(w_tril, v[rows, hcols]) + bias
            mix_scr[rows, d_a + hd * dh_b:d_a + (hd + 1) * dh_b] = (z_u[rows, hcols] * mixed).astype(BF16)

    y = _dot(mix_scr[...], w_out_ref[...])
    o_ref[...] = xf + _rms_norm(y, npost_ref[...])


def _single_buffered(shape):
    return pl.BlockSpec(shape, lambda *_: (0,) * len(shape), pipeline_mode=pl.Buffered(1))


def _ab_layer(x, npre, w_in, wgrp, ascale, lng, lnb, ws, bs, w_out, npost, *, tile):
    bsz, seq, d = x.shape
    d_a = ascale.shape[-1]
    d_b = lng.shape[-1]
    assert seq % tile == 0 and tile % SGU_CHUNK == 0 and tile >= POOL_HALO
    act_spec = pl.BlockSpec((None, tile, d), lambda b, i: (b, i, 0))
    operands = (npre.reshape(1, d), w_in.astype(BF16), wgrp.astype(BF16), ascale.reshape(1, d_a),
                lng.reshape(1, d_b), lnb.reshape(1, d_b), ws, bs.T, w_out.astype(BF16),
                npost.reshape(1, d))
    return pl.pallas_call(
        functools.partial(_ab_body, tile=tile, d_a=d_a, d_b=d_b),
        grid=(bsz, seq // tile),
        in_specs=[act_spec] + [_single_buffered(op.shape) for op in operands],
        out_specs=act_spec,
        out_shape=jax.ShapeDtypeStruct(x.shape, F32),
        scratch_shapes=[pltpu.VMEM((tile + POOL_HALO, d_a), F32), pltpu.VMEM((tile, d_a + d_b), BF16)],
        compiler_params=pltpu.CompilerParams(
            dimension_semantics=("arbitrary", "arbitrary"), vmem_limit_bytes=VMEM_LIMIT_BYTES),
        name="ab_layer",
    )(x, *operands)


CONV_COL_BLOCK = 256
CONV_ROW_BLOCK = 64


def _conv_body(x_ref, npre_ref, w1_ref, b1_ref, wdw_ref, bdw_ref, lng_ref, lnb_ref, w2_ref, b2_ref,
               npost_ref, o_ref, g_scr, c_scr, *, tile, d_c):
    i = pl.program_id(1)
    xf = x_ref[...]
    h = _rms_norm(xf, npre_ref[...]).astype(BF16)

    _carry_halo(g_scr, CONV_HALO, tile, i == 0)
    lin = _dot(h, w1_ref[:, 0:d_c]) + b1_ref[:, 0:d_c]
    gate = _dot(h, w1_ref[:, d_c:2 * d_c]) + b1_ref[:, d_c:2 * d_c]
    g_scr[CONV_HALO:CONV_HALO + tile, :] = lin * jax.nn.sigmoid(gate)

    base = CONV_HALO - (CONV_K - 1)
    for cb in range(d_c // CONV_COL_BLOCK):
        cols = slice(cb * CONV_COL_BLOCK, (cb + 1) * CONV_COL_BLOCK)
        for rb in range(tile // CONV_ROW_BLOCK):
            r0 = rb * CONV_ROW_BLOCK
            acc = wdw_ref[0:1, cols] * g_scr[base + r0:base + r0 + CONV_ROW_BLOCK, cols]
            for k in range(1, CONV_K):
                acc = acc + wdw_ref[k:k + 1, cols] * g_scr[base + k + r0:base + k + r0 + CONV_ROW_BLOCK, cols]
            c_scr[r0:r0 + CONV_ROW_BLOCK, cols] = acc + bdw_ref[:, cols]

    c = jax.nn.silu(_layer_norm(c_scr[...], lng_ref[...], lnb_ref[...])).astype(BF16)
    y = _dot(c, w2_ref[...]) + b2_ref[...]
    o_ref[...] = xf + _rms_norm(y, npost_ref[...])


def _conv_layer(x, npre, w1, b1, wdw, bdw, lng, lnb, w2, b2, npost, *, tile):
    bsz, seq, d = x.shape
    d_c = wdw.shape[-1]
    assert seq % tile == 0 and tile % CONV_ROW_BLOCK == 0 and tile >= CONV_HALO
    assert d_c % CONV_COL_BLOCK == 0 and wdw.shape[0] == CONV_K
    act_spec = pl.BlockSpec((None, tile, d), lambda b, i: (b, i, 0))
    operands = (npre.reshape(1, d), w1.astype(BF16), b1.reshape(1, 2 * d_c), wdw, bdw.reshape(1, d_c),
                lng.reshape(1, d_c), lnb.reshape(1, d_c), w2.astype(BF16), b2.reshape(1, d),
                npost.reshape(1, d))
    return pl.pallas_call(
        functools.partial(_conv_body, tile=tile, d_c=d_c),
        grid=(bsz, seq // tile),
        in_specs=[act_spec] + [_single_buffered(op.shape) for op in operands],
        out_specs=act_spec,
        out_shape=jax.ShapeDtypeStruct(x.shape, F32),
        scratch_shapes=[pltpu.VMEM((tile + CONV_HALO, d_c), F32), pltpu.VMEM((tile, d_c), F32)],
        compiler_params=pltpu.CompilerParams(
            dimension_semantics=("arbitrary", "arbitrary"), vmem_limit_bytes=VMEM_LIMIT_BYTES),
        name="conv_layer",
    )(x, *operands)


def _ffn_body(x_ref, npre_ref, wg_ref, wu_ref, wd_ref, npost_ref, o_ref, h_scr):
    j = pl.program_id(1)

    @pl.when(j == 0)
    def _():
        h_scr[...] = _rms_norm(x_ref[...], npre_ref[...]).astype(BF16)
        o_ref[...] = jnp.zeros(o_ref.shape, o_ref.dtype)

    h = h_scr[...]
    a = jax.nn.silu(_dot(h, wg_ref[...])) * _dot(h, wu_ref[...])
    o_ref[...] += _dot(a.astype(BF16), wd_ref[...])

    @pl.when(j == pl.num_programs(1) - 1)
    def _():
        o_ref[...] = x_ref[...] + _rms_norm(o_ref[...], npost_ref[...])


def _ffn_layer(x, npre, wg, wu, wd, npost, *, tile, ff_tile):
    bsz, seq, d = x.shape
    d_ff = wg.shape[-1]
    tokens = bsz * seq
    assert tokens % tile == 0 and d_ff % ff_tile == 0
    act_spec = pl.BlockSpec((tile, d), lambda i, j: (i, 0))
    vec_spec = pl.BlockSpec((1, d), lambda i, j: (0, 0))
    out = pl.pallas_call(
        _ffn_body,
        grid=(tokens // tile, d_ff // ff_tile),
        in_specs=[act_spec, vec_spec,
                  pl.BlockSpec((d, ff_tile), lambda i, j: (0, j)),
                  pl.BlockSpec((d, ff_tile), lambda i, j: (0, j)),
                  pl.BlockSpec((ff_tile, d), lambda i, j: (j, 0)),
                  vec_spec],
        out_specs=act_spec,
        out_shape=jax.ShapeDtypeStruct((tokens, d), F32),
        scratch_shapes=[pltpu.VMEM((tile, d), BF16)],
        compiler_params=pltpu.CompilerParams(
            dimension_semantics=("arbitrary", "arbitrary"), vmem_limit_bytes=VMEM_LIMIT_BYTES),
        name="ffn_layer",
    )(x.reshape(tokens, d), npre.reshape(1, d), wg.astype(BF16), wu.astype(BF16), wd.astype(BF16),
      npost.reshape(1, d))
    return out.reshape(bsz, seq, d)


AB_TILE = 256
CONV_TILE = 256
FFN_TILE = 512
FFN_FF_TILE = 512


def kernel(x, ab_norm_pre, ab_w_in, a_w_group, a_scale, b_ln_g, b_ln_b, b_w_s, b_b_s, ab_w_out, ab_norm_post, c_norm_pre, c_w_pw1, c_b_pw1, c_w_dw, c_b_dw, c_ln_g, c_ln_b, c_w_pw2, c_b_pw2, c_norm_post, f_norm_pre, f_w_gate, f_w_up, f_w_down, f_norm_post):
    depth = f_norm_pre.shape[0]
    for layer in range(depth):
        i = layer // 2
        if layer % 2 == 0:
            x = _ab_layer(x, ab_norm_pre[i], ab_w_in[i], a_w_group[i], a_scale[i], b_ln_g[i], b_ln_b[i],
                          b_w_s[i], b_b_s[i], ab_w_out[i], ab_norm_post[i], tile=AB_TILE)
        else:
            x = _conv_layer(x, c_norm_pre[i], c_w_pw1[i], c_b_pw1[i], c_w_dw[i], c_b_dw[i], c_ln_g[i],
                            c_ln_b[i], c_w_pw2[i], c_b_pw2[i], c_norm_post[i], tile=CONV_TILE)
        x = _ffn_layer(x, f_norm_pre[layer], f_w_gate[layer], f_w_up[layer], f_w_down[layer],
                       f_norm_post[layer], tile=FFN_TILE, ff_tile=FFN_FF_TILE)
    return x
```

```python
import functools

import jax
import jax.numpy as jnp
from jax import lax
from jax.experimental import pallas as pl
from jax.experimental.pallas import tpu as pltpu

F32 = jnp.float32
BF16 = jnp.bfloat16

EPS = 1e-6
POOL_WINDOWS = (2, 4, 8, 16)
SGU_HEADS = 4
SGU_CHUNK = 128
CONV_K = 31

V7X_VMEM_BYTES = 64 * 1024 * 1024
VMEM_LIMIT_BYTES = V7X_VMEM_BYTES - 8 * 1024 * 1024
LANES = 128
MXU_COLS = 256

POOL_HALO = 16
CONV_HALO = 32


def _rms_norm(xf, g):
    y = xf * lax.rsqrt(jnp.mean(xf * xf, axis=-1, keepdims=True) + EPS)
    return y * g


def _layer_norm(xf, g, b):
    mu = jnp.mean(xf, axis=-1, keepdims=True)
    var = jnp.mean(jnp.square(xf - mu), axis=-1, keepdims=True)
    return (xf - mu) * lax.rsqrt(var + EPS) * g + b


def _dot(a, b):
    return jnp.dot(a, b, preferred_element_type=F32)


def _carry_halo(scr_ref, halo, tile, first_tile):
    @pl.when(first_tile)
    def _():
        scr_ref[:, 0:halo, :] = jnp.zeros((scr_ref.shape[0], halo, LANES), scr_ref.dtype)

    @pl.when(jnp.logical_not(first_tile))
    def _():
        scr_ref[:, 0:halo, :] = scr_ref[:, tile:tile + halo, :]


def _ab_body(x_ref, npre_ref, w_in_ref, wgrp_ref, ascale_ref, lng_ref, lnb_ref, ws_ref, bs_ref,
             w_out_ref, npost_ref, o_ref, a_scr, mix_scr, *, tile, d_a, d_b):
    i = pl.program_id(1)
    g_a = d_a // len(POOL_WINDOWS)
    dh_b = d_b // SGU_HEADS

    xf = x_ref[...]
    h = _rms_norm(xf, npre_ref[...]).astype(BF16)

    _carry_halo(a_scr, POOL_HALO, tile, i == 0)
    a = _dot(h, w_in_ref[:, 0:d_a])
    for s in range(d_a // LANES):
        a_scr[s, POOL_HALO:POOL_HALO + tile, :] = a[:, s * LANES:(s + 1) * LANES]
    pos = i * tile + lax.broadcasted_iota(jnp.int32, (tile, LANES), 0)
    for g, w in enumerate(POOL_WINDOWS):
        cols = slice(g * g_a, (g + 1) * g_a)
        count = jnp.minimum(pos + 1, w).astype(F32)
        d = []
        for s in range(g * g_a // LANES, (g + 1) * g_a // LANES):
            ag = a_scr[s, POOL_HALO:POOL_HALO + tile, :]
            win = ag
            for k in range(1, w):
                win = win + a_scr[s, POOL_HALO - k:POOL_HALO - k + tile, :]
            d.append((win / count - ag).astype(BF16))
        og = _dot(jnp.concatenate(d, axis=-1), wgrp_ref[g])
        mix_scr[:, cols] = (og * ascale_ref[:, cols]).astype(BF16)

    z_u = jax.nn.gelu(_dot(h, w_in_ref[:, d_a:d_a + d_b]))
    z_v = jax.nn.gelu(_dot(h, w_in_ref[:, d_a + d_b:d_a + 2 * d_b]))
    v = _layer_norm(z_v, lng_ref[...], lnb_ref[...]).astype(BF16)
    row = lax.broadcasted_iota(jnp.int32, (SGU_CHUNK, SGU_CHUNK), 0)
    col = lax.broadcasted_iota(jnp.int32, (SGU_CHUNK, SGU_CHUNK), 1)
    for hd in range(SGU_HEADS):
        w_tril = jnp.where(col <= row, ws_ref[hd], 0.0).astype(BF16)
        bias = bs_ref[:, hd:hd + 1]
        hcols = slice(hd * dh_b, (hd + 1) * dh_b)
        for c in range(tile // SGU_CHUNK):
            rows = slice(c * SGU_CHUNK, (c + 1) * SGU_CHUNK)
            mixed = _dot(w_tril, v[rows, hcols]) + bias
            mix_scr[rows, d_a + hd * dh_b:d_a + (hd + 1) * dh_b] = (z_u[rows, hcols] * mixed).astype(BF16)

    y = _dot(mix_scr[...], w_out_ref[...])
    o_ref[...] = xf + _rms_norm(y, npost_ref[...])


def _single_buffered(shape):
    return pl.BlockSpec(shape, lambda *_: (0,) * len(shape), pipeline_mode=pl.Buffered(1))


def _ab_layer(x, npre, w_in, wgrp, ascale, lng, lnb, ws, bs, w_out, npost, *, tile):
    bsz, seq, d = x.shape
    d_a = ascale.shape[-1]
    d_b = lng.shape[-1]
    assert seq % tile == 0 and tile % SGU_CHUNK == 0 and tile >= POOL_HALO
    assert (d_a // len(POOL_WINDOWS)) % LANES == 0
    act_spec = pl.BlockSpec((None, tile, d), lambda b, i: (b, i, 0))
    operands = (npre.reshape(1, d), w_in.astype(BF16), wgrp.astype(BF16), ascale.reshape(1, d_a),
                lng.reshape(1, d_b), lnb.reshape(1, d_b), ws, bs.T, w_out.astype(BF16),
                npost.reshape(1, d))
    return pl.pallas_call(
        functools.partial(_ab_body, tile=tile, d_a=d_a, d_b=d_b),
        grid=(bsz, seq // tile),
        in_specs=[act_spec] + [_single_buffered(op.shape) for op in operands],
        out_specs=act_spec,
        out_shape=jax.ShapeDtypeStruct(x.shape, F32),
        scratch_shapes=[pltpu.VMEM((d_a // LANES, tile + POOL_HALO, LANES), F32),
                        pltpu.VMEM((tile, d_a + d_b), BF16)],
        compiler_params=pltpu.CompilerParams(
            dimension_semantics=("arbitrary", "arbitrary"), vmem_limit_bytes=VMEM_LIMIT_BYTES),
        name="ab_layer",
    )(x, *operands)


CONV_ROW_BLOCK = 128


def _conv_body(x_ref, npre_ref, w1_ref, b1_ref, wdw_ref, bdw_ref, lng_ref, lnb_ref, w2_ref, b2_ref,
               npost_ref, o_ref, g_scr, c_scr, *, tile, d_c):
    i = pl.program_id(1)
    xf = x_ref[...]
    h = _rms_norm(xf, npre_ref[...]).astype(BF16)

    _carry_halo(g_scr, CONV_HALO, tile, i == 0)
    base = CONV_HALO - (CONV_K - 1)
    for cb in range(d_c // MXU_COLS):
        lin_cols = slice(cb * MXU_COLS, (cb + 1) * MXU_COLS)
        gate_cols = slice(d_c + cb * MXU_COLS, d_c + (cb + 1) * MXU_COLS)
        lin = _dot(h, w1_ref[:, lin_cols]) + b1_ref[:, lin_cols]
        gate = _dot(h, w1_ref[:, gate_cols]) + b1_ref[:, gate_cols]
        glu = lin * jax.nn.sigmoid(gate)
        for half in range(MXU_COLS // LANES):
            s = cb * (MXU_COLS // LANES) + half
            cols = slice(s * LANES, (s + 1) * LANES)
            g_scr[s, CONV_HALO:CONV_HALO + tile, :] = glu[:, half * LANES:(half + 1) * LANES]
            for r0 in range(0, tile, CONV_ROW_BLOCK):
                acc = wdw_ref[0:1, cols] * g_scr[s, base + r0:base + r0 + CONV_ROW_BLOCK, :]
                for k in range(1, CONV_K):
                    acc = acc + wdw_ref[k:k + 1, cols] * g_scr[s, base + k + r0:base + k + r0 + CONV_ROW_BLOCK, :]
                c_scr[r0:r0 + CONV_ROW_BLOCK, cols] = acc + bdw_ref[:, cols]

    c = jax.nn.silu(_layer_norm(c_scr[...], lng_ref[...], lnb_ref[...])).astype(BF16)
    y = _dot(c, w2_ref[...]) + b2_ref[...]
    o_ref[...] = xf + _rms_norm(y, npost_ref[...])


def _conv_layer(x, npre, w1, b1, wdw, bdw, lng, lnb, w2, b2, npost, *, tile):
    bsz, seq, d = x.shape
    d_c = wdw.shape[-1]
    assert seq % tile == 0 and tile % CONV_ROW_BLOCK == 0 and tile >= CONV_HALO
    assert d_c % MXU_COLS == 0 and wdw.shape[0] == CONV_K
    act_spec = pl.BlockSpec((None, tile, d), lambda b, i: (b, i, 0))
    operands = (npre.reshape(1, d), w1.astype(BF16), b1.reshape(1, 2 * d_c), wdw, bdw.reshape(1, d_c),
                lng.reshape(1, d_c), lnb.reshape(1, d_c), w2.astype(BF16), b2.reshape(1, d),
                npost.reshape(1, d))
    return pl.pallas_call(
        functools.partial(_conv_body, tile=tile, d_c=d_c),
        grid=(bsz, seq // tile),
        in_specs=[act_spec] + [_single_buffered(op.shape) for op in operands],
        out_specs=act_spec,
        out_shape=jax.ShapeDtypeStruct(x.shape, F32),
        scratch_shapes=[pltpu.VMEM((d_c // LANES, tile + CONV_HALO, LANES), F32),
                        pltpu.VMEM((tile, d_c), F32)],
        compiler_params=pltpu.CompilerParams(
            dimension_semantics=("arbitrary", "arbitrary"), vmem_limit_bytes=VMEM_LIMIT_BYTES),
        name="conv_layer",
    )(x, *operands)


def _ffn_body(x_ref, npre_ref, wg_ref, wu_ref, wd_ref, npost_ref, o_ref, h_scr):
    j = pl.program_id(1)

    @pl.when(j == 0)
    def _():
        h_scr[...] = _rms_norm(x_ref[...], npre_ref[...]).astype(BF16)
        o_ref[...] = jnp.zeros(o_ref.shape, o_ref.dtype)

    h = h_scr[...]
    a = jax.nn.silu(_dot(h, wg_ref[...])) * _dot(h, wu_ref[...])
    o_ref[...] += _dot(a.astype(BF16), wd_ref[...])

    @pl.when(j == pl.num_programs(1) - 1)
    def _():
        o_ref[...] = x_ref[...] + _rms_norm(o_ref[...], npost_ref[...])


def _ffn_layer(x, npre, wg, wu, wd, npost, *, tile, ff_tile):
    bsz, seq, d = x.shape
    d_ff = wg.shape[-1]
    tokens = bsz * seq
    assert tokens % tile == 0 and d_ff % ff_tile == 0
    act_spec = pl.BlockSpec((tile, d), lambda i, j: (i, 0))
    vec_spec = pl.BlockSpec((1, d), lambda i, j: (0, 0))
    out = pl.pallas_call(
        _ffn_body,
        grid=(tokens // tile, d_ff // ff_tile),
        in_specs=[act_spec, vec_spec,
                  pl.BlockSpec((d, ff_tile), lambda i, j: (0, j)),
                  pl.BlockSpec((d, ff_tile), lambda i, j: (0, j)),
                  pl.BlockSpec((ff_tile, d), lambda i, j: (j, 0)),
                  vec_spec],
        out_specs=act_spec,
        out_shape=jax.ShapeDtypeStruct((tokens, d), F32),
        scratch_shapes=[pltpu.VMEM((tile, d), BF16)],
        compiler_params=pltpu.CompilerParams(
            dimension_semantics=("arbitrary", "arbitrary"), vmem_limit_bytes=VMEM_LIMIT_BYTES),
        name="ffn_layer",
    )(x.reshape(tokens, d), npre.reshape(1, d), wg.astype(BF16), wu.astype(BF16), wd.astype(BF16),
      npost.reshape(1, d))
    return out.reshape(bsz, seq, d)


AB_TILE = 256
CONV_TILE = 256
FFN_TILE = 512
FFN_FF_TILE = 512


def kernel(x, ab_norm_pre, ab_w_in, a_w_group, a_scale, b_ln_g, b_ln_b, b_w_s, b_b_s, ab_w_out, ab_norm_post, c_norm_pre, c_w_pw1, c_b_pw1, c_w_dw, c_b_dw, c_ln_g, c_ln_b, c_w_pw2, c_b_pw2, c_norm_post, f_norm_pre, f_w_gate, f_w_up, f_w_down, f_norm_post):
    depth = f_norm_pre.shape[0]
    for layer in range(depth):
        i = layer // 2
        if layer % 2 == 0:
            x = _ab_layer(x, ab_norm_pre[i], ab_w_in[i], a_w_group[i], a_scale[i], b_ln_g[i], b_ln_b[i],
                          b_w_s[i], b_b_s[i], ab_w_out[i], ab_norm_post[i], tile=AB_TILE)
        else:
            x = _conv_layer(x, c_norm_pre[i], c_w_pw1[i], c_b_pw1[i], c_w_dw[i], c_b_dw[i], c_ln_g[i],
                            c_ln_b[i], c_w_pw2[i], c_b_pw2[i], c_norm_post[i], tile=CONV_TILE)
        x = _ffn_layer(x, f_norm_pre[layer], f_w_gate[layer], f_w_up[layer], f_w_down[layer],
                       f_norm_post[layer], tile=FFN_TILE, ff_tile=FFN_FF_TILE)
    return x
```

```python
import functools

import jax
import jax.numpy as jnp
from jax import lax
from jax.experimental import pallas as pl
from jax.experimental.pallas import tpu as pltpu

F32 = jnp.float32
BF16 = jnp.bfloat16

EPS = 1e-6
POOL_WINDOWS = (2, 4, 8, 16)
SGU_HEADS = 4
SGU_CHUNK = 128
CONV_K = 31

V7X_VMEM_BYTES = 64 * 1024 * 1024
VMEM_LIMIT_BYTES = V7X_VMEM_BYTES - 3 * 1024 * 1024
LANES = 128
MXU_COLS = 256

POOL_HALO = 16
CONV_HALO = 32


def _rms_norm(xf, g):
    y = xf * lax.rsqrt(jnp.mean(xf * xf, axis=-1, keepdims=True) + EPS)
    return y * g


def _layer_norm(xf, g, b):
    mu = jnp.mean(xf, axis=-1, keepdims=True)
    var = jnp.mean(jnp.square(xf - mu), axis=-1, keepdims=True)
    return (xf - mu) * lax.rsqrt(var + EPS) * g + b


def _dot(a, b):
    return jnp.dot(a, b, preferred_element_type=F32)


def _carry_halo(scr_ref, halo, tile, first_tile):
    @pl.when(first_tile)
    def _():
        scr_ref[:, 0:halo, :] = jnp.zeros((scr_ref.shape[0], halo, LANES), scr_ref.dtype)

    @pl.when(jnp.logical_not(first_tile))
    def _():
        scr_ref[:, 0:halo, :] = scr_ref[:, tile:tile + halo, :]


def _ab_body(x_ref, npre_ref, w_in_ref, wgrp_ref, ascale_ref, lng_ref, lnb_ref, ws_ref, bs_ref,
             w_out_ref, npost_ref, o_ref, a_scr, mix_scr, *, tile, d_a, d_b):
    i = pl.program_id(1)
    g_a = d_a // len(POOL_WINDOWS)
    dh_b = d_b // SGU_HEADS

    xf = x_ref[...]
    h = _rms_norm(xf, npre_ref[...]).astype(BF16)

    _carry_halo(a_scr, POOL_HALO, tile, i == 0)
    a = _dot(h, w_in_ref[:, 0:d_a])
    for s in range(d_a // LANES):
        a_scr[s, POOL_HALO:POOL_HALO + tile, :] = a[:, s * LANES:(s + 1) * LANES]
    pos = i * tile + lax.broadcasted_iota(jnp.int32, (tile, LANES), 0)
    for g, w in enumerate(POOL_WINDOWS):
        cols = slice(g * g_a, (g + 1) * g_a)
        count = jnp.minimum(pos + 1, w).astype(F32)
        d = []
        for s in range(g * g_a // LANES, (g + 1) * g_a // LANES):
            ag = a_scr[s, POOL_HALO:POOL_HALO + tile, :]
            win = ag
            for k in range(1, w):
                win = win + a_scr[s, POOL_HALO - k:POOL_HALO - k + tile, :]
            d.append((win / count - ag).astype(BF16))
        og = _dot(jnp.concatenate(d, axis=-1), wgrp_ref[g])
        mix_scr[:, cols] = (og * ascale_ref[:, cols]).astype(BF16)

    z_u = jax.nn.gelu(_dot(h, w_in_ref[:, d_a:d_a + d_b]))
    z_v = jax.nn.gelu(_dot(h, w_in_ref[:, d_a + d_b:d_a + 2 * d_b]))
    v = _layer_norm(z_v, lng_ref[...], lnb_ref[...]).astype(BF16)
    row = lax.broadcasted_iota(jnp.int32, (SGU_CHUNK, SGU_CHUNK), 0)
    col = lax.broadcasted_iota(jnp.int32, (SGU_CHUNK, SGU_CHUNK), 1)
    for hd in range(SGU_HEADS):
        w_tril = jnp.where(col <= row, ws_ref[hd], 0.0).astype(BF16)
        bias = bs_ref[:, hd:hd + 1]
        hcols = slice(hd * dh_b, (hd + 1) * dh_b)
        for c in range(tile // SGU_CHUNK):
            rows = slice(c * SGU_CHUNK, (c + 1) * SGU_CHUNK)
            mixed = _dot(w_tril, v[rows, hcols]) + bias
            mix_scr[rows, d_a + hd * dh_b:d_a + (hd + 1) * dh_b] = (z_u[rows, hcols] * mixed).astype(BF16)

    y = _dot(mix_scr[...], w_out_ref[...])
    o_ref[...] = xf + _rms_norm(y, npost_ref[...])


def _single_buffered(shape):
    return pl.BlockSpec(shape, lambda *_: (0,) * len(shape), pipeline_mode=pl.Buffered(1))


def _ab_layer(x, npre, w_in, wgrp, ascale, lng, lnb, ws, bs, w_out, npost, *, tile):
    bsz, seq, d = x.shape
    d_a = ascale.shape[-1]
    d_b = lng.shape[-1]
    assert seq % tile == 0 and tile % SGU_CHUNK == 0 and tile >= POOL_HALO
    assert (d_a // len(POOL_WINDOWS)) % LANES == 0
    act_spec = pl.BlockSpec((None, tile, d), lambda b, i: (b, i, 0))
    operands = (npre.reshape(1, d), w_in.astype(BF16), wgrp.astype(BF16), ascale.reshape(1, d_a),
                lng.reshape(1, d_b), lnb.reshape(1, d_b), ws, bs.T, w_out.astype(BF16),
                npost.reshape(1, d))
    return pl.pallas_call(
        functools.partial(_ab_body, tile=tile, d_a=d_a, d_b=d_b),
        grid=(bsz, seq // tile),
        in_specs=[act_spec] + [_single_buffered(op.shape) for op in operands],
        out_specs=act_spec,
        out_shape=jax.ShapeDtypeStruct(x.shape, F32),
        scratch_shapes=[pltpu.VMEM((d_a // LANES, tile + POOL_HALO, LANES), F32),
                        pltpu.VMEM((tile, d_a + d_b), BF16)],
        compiler_params=pltpu.CompilerParams(
            dimension_semantics=("arbitrary", "arbitrary"), vmem_limit_bytes=VMEM_LIMIT_BYTES),
        name="ab_layer",
    )(x, *operands)


CONV_ROW_BLOCK = 128


def _conv_body(x_ref, npre_ref, w1_ref, b1_ref, wdw_ref, bdw_ref, lng_ref, lnb_ref, w2_ref, b2_ref,
               npost_ref, o_ref, g_scr, c_scr, *, tile, d_c):
    i = pl.program_id(1)
    xf = x_ref[...]
    h = _rms_norm(xf, npre_ref[...]).astype(BF16)

    _carry_halo(g_scr, CONV_HALO, tile, i == 0)
    base = CONV_HALO - (CONV_K - 1)
    for cb in range(d_c // MXU_COLS):
        lin_cols = slice(cb * MXU_COLS, (cb + 1) * MXU_COLS)
        gate_cols = slice(d_c + cb * MXU_COLS, d_c + (cb + 1) * MXU_COLS)
        lin = _dot(h, w1_ref[:, lin_cols]) + b1_ref[:, lin_cols]
        gate = _dot(h, w1_ref[:, gate_cols]) + b1_ref[:, gate_cols]
        glu = lin * jax.nn.sigmoid(gate)
        for half in range(MXU_COLS // LANES):
            s = cb * (MXU_COLS // LANES) + half
            cols = slice(s * LANES, (s + 1) * LANES)
            g_scr[s, CONV_HALO:CONV_HALO + tile, :] = glu[:, half * LANES:(half + 1) * LANES]
            for r0 in range(0, tile, CONV_ROW_BLOCK):
                acc = wdw_ref[0:1, cols] * g_scr[s, base + r0:base + r0 + CONV_ROW_BLOCK, :]
                for k in range(1, CONV_K):
                    acc = acc + wdw_ref[k:k + 1, cols] * g_scr[s, base + k + r0:base + k + r0 + CONV_ROW_BLOCK, :]
                c_scr[r0:r0 + CONV_ROW_BLOCK, cols] = acc + bdw_ref[:, cols]

    c = jax.nn.silu(_layer_norm(c_scr[...], lng_ref[...], lnb_ref[...])).astype(BF16)
    y = _dot(c, w2_ref[...]) + b2_ref[...]
    o_ref[...] = xf + _rms_norm(y, npost_ref[...])


def _conv_layer(x, npre, w1, b1, wdw, bdw, lng, lnb, w2, b2, npost, *, tile):
    bsz, seq, d = x.shape
    d_c = wdw.shape[-1]
    assert seq % tile == 0 and tile % CONV_ROW_BLOCK == 0 and tile >= CONV_HALO
    assert d_c % MXU_COLS == 0 and wdw.shape[0] == CONV_K
    act_spec = pl.BlockSpec((None, tile, d), lambda b, i: (b, i, 0))
    operands = (npre.reshape(1, d), w1.astype(BF16), b1.reshape(1, 2 * d_c), wdw, bdw.reshape(1, d_c),
                lng.reshape(1, d_c), lnb.reshape(1, d_c), w2.astype(BF16), b2.reshape(1, d),
                npost.reshape(1, d))
    return pl.pallas_call(
        functools.partial(_conv_body, tile=tile, d_c=d_c),
        grid=(bsz, seq // tile),
        in_specs=[act_spec] + [_single_buffered(op.shape) for op in operands],
        out_specs=act_spec,
        out_shape=jax.ShapeDtypeStruct(x.shape, F32),
        scratch_shapes=[pltpu.VMEM((d_c // LANES, tile + CONV_HALO, LANES), F32),
                        pltpu.VMEM((tile, d_c), F32)],
        compiler_params=pltpu.CompilerParams(
            dimension_semantics=("arbitrary", "arbitrary"), vmem_limit_bytes=VMEM_LIMIT_BYTES),
        name="conv_layer",
    )(x, *operands)


FFN_ROW_CHUNK = 256


def _ffn_body(x_ref, npre_ref, wg_ref, wu_ref, wd_ref, npost_ref, o_ref, h_scr):
    j = pl.program_id(1)
    last = pl.num_programs(1) - 1
    tile = o_ref.shape[0]

    def swiglu(rows):
        h = h_scr[rows, :]
        a = jax.nn.silu(_dot(h, wg_ref[...])) * _dot(h, wu_ref[...])
        return _dot(a.astype(BF16), wd_ref[...])

    @pl.when(j == 0)
    def _():
        for r0 in range(0, tile, FFN_ROW_CHUNK):
            rows = slice(r0, r0 + FFN_ROW_CHUNK)
            h_scr[rows, :] = _rms_norm(x_ref[rows, :], npre_ref[...]).astype(BF16)
            o_ref[rows, :] = swiglu(rows)

    @pl.when(jnp.logical_and(j > 0, j < last))
    def _():
        o_ref[...] += swiglu(slice(None))

    @pl.when(j == last)
    def _():
        for r0 in range(0, tile, FFN_ROW_CHUNK):
            rows = slice(r0, r0 + FFN_ROW_CHUNK)
            y = o_ref[rows, :] + swiglu(rows)
            o_ref[rows, :] = x_ref[rows, :] + _rms_norm(y, npost_ref[...])


def _ffn_layer(x, npre, wg, wu, wd, npost, *, tile, ff_tile):
    bsz, seq, d = x.shape
    d_ff = wg.shape[-1]
    tokens = bsz * seq
    assert tokens % tile == 0 and tile % FFN_ROW_CHUNK == 0
    assert d_ff % ff_tile == 0 and d_ff // ff_tile >= 2
    act_spec = pl.BlockSpec((tile, d), lambda i, j: (i, 0))
    vec_spec = pl.BlockSpec((1, d), lambda i, j: (0, 0))
    out = pl.pallas_call(
        _ffn_body,
        grid=(tokens // tile, d_ff // ff_tile),
        in_specs=[act_spec, vec_spec,
                  pl.BlockSpec((d, ff_tile), lambda i, j: (0, j)),
                  pl.BlockSpec((d, ff_tile), lambda i, j: (0, j)),
                  pl.BlockSpec((ff_tile, d), lambda i, j: (j, 0)),
                  vec_spec],
        out_specs=act_spec,
        out_shape=jax.ShapeDtypeStruct((tokens, d), F32),
        scratch_shapes=[pltpu.VMEM((tile, d), BF16)],
        compiler_params=pltpu.CompilerParams(
            dimension_semantics=("arbitrary", "arbitrary"), vmem_limit_bytes=VMEM_LIMIT_BYTES),
        name="ffn_layer",
    )(x.reshape(tokens, d), npre.reshape(1, d), wg.astype(BF16), wu.astype(BF16), wd.astype(BF16),
      npost.reshape(1, d))
    return out.reshape(bsz, seq, d)


AB_TILE = 256
CONV_TILE = 256
FFN_TILE = 1024
FFN_FF_TILE = 512


def kernel(x, ab_norm_pre, ab_w_in, a_w_group, a_scale, b_ln_g, b_ln_b, b_w_s, b_b_s, ab_w_out, ab_norm_post, c_norm_pre, c_w_pw1, c_b_pw1, c_w_dw, c_b_dw, c_ln_g, c_ln_b, c_w_pw2, c_b_pw2, c_norm_post, f_norm_pre, f_w_gate, f_w_up, f_w_down, f_norm_post):
    depth = f_norm_pre.shape[0]
    for layer in range(depth):
        i = layer // 2
        if layer % 2 == 0:
            x = _ab_layer(x, ab_norm_pre[i], ab_w_in[i], a_w_group[i], a_scale[i], b_ln_g[i], b_ln_b[i],
                          b_w_s[i], b_b_s[i], ab_w_out[i], ab_norm_post[i], tile=AB_TILE)
        else:
            x = _conv_layer(x, c_norm_pre[i], c_w_pw1[i], c_b_pw1[i], c_w_dw[i], c_b_dw[i], c_ln_g[i],
                            c_ln_b[i], c_w_pw2[i], c_b_pw2[i], c_norm_post[i], tile=CONV_TILE)
        x = _ffn_layer(x, f_norm_pre[layer], f_w_gate[layer], f_w_up[layer], f_w_down[layer],
                       f_norm_post[layer], tile=FFN_TILE, ff_tile=FFN_FF_TILE)
    return x
```

```python
import functools

import jax
import jax.numpy as jnp
from jax import lax
from jax.experimental import pallas as pl
from jax.experimental.pallas import tpu as pltpu

F32 = jnp.float32
BF16 = jnp.bfloat16

EPS = 1e-6
POOL_WINDOWS = (2, 4, 8, 16)
SGU_HEADS = 4
SGU_CHUNK = 128
CONV_K = 31

V7X_VMEM_BYTES = 64 * 1024 * 1024
VMEM_LIMIT_BYTES = V7X_VMEM_BYTES - 3 * 1024 * 1024
LANES = 128
MXU_COLS = 256

POOL_HALO = 16
CONV_HALO = 32


def _rms_norm(xf, g):
    y = xf * lax.rsqrt(jnp.mean(xf * xf, axis=-1, keepdims=True) + EPS)
    return y * g


def _layer_norm(xf, g, b):
    mu = jnp.mean(xf, axis=-1, keepdims=True)
    var = jnp.mean(jnp.square(xf - mu), axis=-1, keepdims=True)
    return (xf - mu) * lax.rsqrt(var + EPS) * g + b


def _dot(a, b):
    return jnp.dot(a, b, preferred_element_type=F32)


def _pack_weight(w):
    bits = lax.bitcast_convert_type(w.astype(BF16), jnp.uint16).astype(jnp.uint32)
    return bits[0::2] | (bits[1::2] << 16)


def _weight(w_ref, cols=slice(None)):
    return pltpu.bitcast(w_ref[:, cols], BF16)


def _carry_halo(scr_ref, halo, tile, first_tile):
    @pl.when(first_tile)
    def _():
        scr_ref[:, 0:halo, :] = jnp.zeros((scr_ref.shape[0], halo, LANES), scr_ref.dtype)

    @pl.when(jnp.logical_not(first_tile))
    def _():
        scr_ref[:, 0:halo, :] = scr_ref[:, tile:tile + halo, :]


def _ab_body(x_ref, npre_ref, w_in_ref, wgrp_ref, ascale_ref, lng_ref, lnb_ref, ws_ref, bs_ref,
             w_out_ref, npost_ref, o_ref, a_scr, mix_scr, *, tile, d_a, d_b):
    i = pl.program_id(1)
    g_a = d_a // len(POOL_WINDOWS)
    dh_b = d_b // SGU_HEADS

    xf = x_ref[...]
    h = _rms_norm(xf, npre_ref[...]).astype(BF16)

    _carry_halo(a_scr, POOL_HALO, tile, i == 0)
    a = _dot(h, w_in_ref[:, 0:d_a])
    for s in range(d_a // LANES):
        a_scr[s, POOL_HALO:POOL_HALO + tile, :] = a[:, s * LANES:(s + 1) * LANES]
    pos = i * tile + lax.broadcasted_iota(jnp.int32, (tile, LANES), 0)
    for g, w in enumerate(POOL_WINDOWS):
        cols = slice(g * g_a, (g + 1) * g_a)
        count = jnp.minimum(pos + 1, w).astype(F32)
        d = []
        for s in range(g * g_a // LANES, (g + 1) * g_a // LANES):
            ag = a_scr[s, POOL_HALO:POOL_HALO + tile, :]
            win = ag
            for k in range(1, w):
                win = win + a_scr[s, POOL_HALO - k:POOL_HALO - k + tile, :]
            d.append((win / count - ag).astype(BF16))
        og = _dot(jnp.concatenate(d, axis=-1), wgrp_ref[g])
        mix_scr[:, cols] = (og * ascale_ref[:, cols]).astype(BF16)

    z_u = jax.nn.gelu(_dot(h, w_in_ref[:, d_a:d_a + d_b]))
    z_v = jax.nn.gelu(_dot(h, w_in_ref[:, d_a + d_b:d_a + 2 * d_b]))
    v = _layer_norm(z_v, lng_ref[...], lnb_ref[...]).astype(BF16)
    row = lax.broadcasted_iota(jnp.int32, (SGU_CHUNK, SGU_CHUNK), 0)
    col = lax.broadcasted_iota(jnp.int32, (SGU_CHUNK, SGU_CHUNK), 1)
    for hd in range(SGU_HEADS):
        w_tril = jnp.where(col <= row, ws_ref[hd], 0.0).astype(BF16)
        bias = bs_ref[:, hd:hd + 1]
        hcols = slice(hd * dh_b, (hd + 1) * dh_b)
        for c in range(tile // SGU_CHUNK):
            rows = slice(c * SGU_CHUNK, (c + 1) * SGU_CHUNK)
            mixed = _dot(w_tril, v[rows, hcols]) + bias
            mix_scr[rows, d_a + hd * dh_b:d_a + (hd + 1) * dh_b] = (z_u[rows, hcols] * mixed).astype(BF16)

    y = _dot(mix_scr[...], w_out_ref[...])
    o_ref[...] = xf + _rms_norm(y, npost_ref[...])


def _single_buffered(shape):
    return pl.BlockSpec(shape, lambda *_: (0,) * len(shape), pipeline_mode=pl.Buffered(1))


def _ab_layer(x, npre, w_in, wgrp, ascale, lng, lnb, ws, bs, w_out, npost, *, tile):
    bsz, seq, d = x.shape
    d_a = ascale.shape[-1]
    d_b = lng.shape[-1]
    assert seq % tile == 0 and tile % SGU_CHUNK == 0 and tile >= POOL_HALO
    assert (d_a // len(POOL_WINDOWS)) % LANES == 0
    act_spec = pl.BlockSpec((None, tile, d), lambda b, i: (b, i, 0))
    operands = (npre.reshape(1, d), w_in.astype(BF16), wgrp.astype(BF16), ascale.reshape(1, d_a),
                lng.reshape(1, d_b), lnb.reshape(1, d_b), ws, bs.T, w_out.astype(BF16),
                npost.reshape(1, d))
    return pl.pallas_call(
        functools.partial(_ab_body, tile=tile, d_a=d_a, d_b=d_b),
        grid=(bsz, seq // tile),
        in_specs=[act_spec] + [_single_buffered(op.shape) for op in operands],
        out_specs=act_spec,
        out_shape=jax.ShapeDtypeStruct(x.shape, F32),
        scratch_shapes=[pltpu.VMEM((d_a // LANES, tile + POOL_HALO, LANES), F32),
                        pltpu.VMEM((tile, d_a + d_b), BF16)],
        compiler_params=pltpu.CompilerParams(
            dimension_semantics=("arbitrary", "arbitrary"), vmem_limit_bytes=VMEM_LIMIT_BYTES),
        name="ab_layer",
    )(x, *operands)


CONV_ROW_BLOCK = 128


def _conv_body(x_ref, npre_ref, w1_ref, b1_ref, wdw_ref, bdw_ref, lng_ref, lnb_ref, w2_ref, b2_ref,
               npost_ref, o_ref, g_scr, c_scr, *, tile, d_c):
    i = pl.program_id(1)
    xf = x_ref[...]
    h = _rms_norm(xf, npre_ref[...]).astype(BF16)

    _carry_halo(g_scr, CONV_HALO, tile, i == 0)
    base = CONV_HALO - (CONV_K - 1)
    for cb in range(d_c // MXU_COLS):
        lin_cols = slice(cb * MXU_COLS, (cb + 1) * MXU_COLS)
        gate_cols = slice(d_c + cb * MXU_COLS, d_c + (cb + 1) * MXU_COLS)
        lin = _dot(h, _weight(w1_ref, lin_cols)) + b1_ref[:, lin_cols]
        gate = _dot(h, _weight(w1_ref, gate_cols)) + b1_ref[:, gate_cols]
        glu = lin * jax.nn.sigmoid(gate)
        for half in range(MXU_COLS // LANES):
            s = cb * (MXU_COLS // LANES) + half
            cols = slice(s * LANES, (s + 1) * LANES)
            g_scr[s, CONV_HALO:CONV_HALO + tile, :] = glu[:, half * LANES:(half + 1) * LANES]
            for r0 in range(0, tile, CONV_ROW_BLOCK):
                acc = wdw_ref[0:1, cols] * g_scr[s, base + r0:base + r0 + CONV_ROW_BLOCK, :]
                for k in range(1, CONV_K):
                    acc = acc + wdw_ref[k:k + 1, cols] * g_scr[s, base + k + r0:base + k + r0 + CONV_ROW_BLOCK, :]
                c_scr[r0:r0 + CONV_ROW_BLOCK, cols] = acc + bdw_ref[:, cols]

    c = jax.nn.silu(_layer_norm(c_scr[...], lng_ref[...], lnb_ref[...])).astype(BF16)
    y = _dot(c, _weight(w2_ref)) + b2_ref[...]
    o_ref[...] = xf + _rms_norm(y, npost_ref[...])


def _conv_layer(x, npre, w1, b1, wdw, bdw, lng, lnb, w2, b2, npost, *, tile):
    bsz, seq, d = x.shape
    d_c = wdw.shape[-1]
    assert seq % tile == 0 and tile % CONV_ROW_BLOCK == 0 and tile >= CONV_HALO
    assert d_c % MXU_COLS == 0 and wdw.shape[0] == CONV_K
    act_spec = pl.BlockSpec((None, tile, d), lambda b, i: (b, i, 0))
    operands = (npre.reshape(1, d), _pack_weight(w1), b1.reshape(1, 2 * d_c), wdw, bdw.reshape(1, d_c),
                lng.reshape(1, d_c), lnb.reshape(1, d_c), _pack_weight(w2), b2.reshape(1, d),
                npost.reshape(1, d))
    return pl.pallas_call(
        functools.partial(_conv_body, tile=tile, d_c=d_c),
        grid=(bsz, seq // tile),
        in_specs=[act_spec] + [_single_buffered(op.shape) for op in operands],
        out_specs=act_spec,
        out_shape=jax.ShapeDtypeStruct(x.shape, F32),
        scratch_shapes=[pltpu.VMEM((d_c // LANES, tile + CONV_HALO, LANES), F32),
                        pltpu.VMEM((tile, d_c), F32)],
        compiler_params=pltpu.CompilerParams(
            dimension_semantics=("arbitrary", "arbitrary"), vmem_limit_bytes=VMEM_LIMIT_BYTES),
        name="conv_layer",
    )(x, *operands)


FFN_ROW_CHUNK = 256


def _ffn_body(x_ref, npre_ref, wg_ref, wu_ref, wd_ref, npost_ref, o_ref, h_scr):
    j = pl.program_id(1)
    last = pl.num_programs(1) - 1
    tile = o_ref.shape[0]

    def swiglu(rows):
        h = h_scr[rows, :]
        a = jax.nn.silu(_dot(h, wg_ref[...])) * _dot(h, wu_ref[...])
        return _dot(a.astype(BF16), wd_ref[...])

    @pl.when(j == 0)
    def _():
        for r0 in range(0, tile, FFN_ROW_CHUNK):
            rows = slice(r0, r0 + FFN_ROW_CHUNK)
            h_scr[rows, :] = _rms_norm(x_ref[rows, :], npre_ref[...]).astype(BF16)
            o_ref[rows, :] = swiglu(rows)

    @pl.when(jnp.logical_and(j > 0, j < last))
    def _():
        o_ref[...] += swiglu(slice(None))

    @pl.when(j == last)
    def _():
        for r0 in range(0, tile, FFN_ROW_CHUNK):
            rows = slice(r0, r0 + FFN_ROW_CHUNK)
            y = o_ref[rows, :] + swiglu(rows)
            o_ref[rows, :] = x_ref[rows, :] + _rms_norm(y, npost_ref[...])


def _ffn_layer(x, npre, wg, wu, wd, npost, *, tile, ff_tile):
    bsz, seq, d = x.shape
    d_ff = wg.shape[-1]
    tokens = bsz * seq
    assert tokens % tile == 0 and tile % FFN_ROW_CHUNK == 0
    assert d_ff % ff_tile == 0 and d_ff // ff_tile >= 2
    act_spec = pl.BlockSpec((tile, d), lambda i, j: (i, 0))
    vec_spec = pl.BlockSpec((1, d), lambda i, j: (0, 0))
    out = pl.pallas_call(
        _ffn_body,
        grid=(tokens // tile, d_ff // ff_tile),
        in_specs=[act_spec, vec_spec,
                  pl.BlockSpec((d, ff_tile), lambda i, j: (0, j)),
                  pl.BlockSpec((d, ff_tile), lambda i, j: (0, j)),
                  pl.BlockSpec((ff_tile, d), lambda i, j: (j, 0)),
                  vec_spec],
        out_specs=act_spec,
        out_shape=jax.ShapeDtypeStruct((tokens, d), F32),
        scratch_shapes=[pltpu.VMEM((tile, d), BF16)],
        compiler_params=pltpu.CompilerParams(
            dimension_semantics=("arbitrary", "arbitrary"), vmem_limit_bytes=VMEM_LIMIT_BYTES),
        name="ffn_layer",
    )(x.reshape(tokens, d), npre.reshape(1, d), wg.astype(BF16), wu.astype(BF16), wd.astype(BF16),
      npost.reshape(1, d))
    return out.reshape(bsz, seq, d)


AB_TILE = 256
CONV_TILE = 256
FFN_TILE = 1024
FFN_FF_TILE = 512


def kernel(x, ab_norm_pre, ab_w_in, a_w_group, a_scale, b_ln_g, b_ln_b, b_w_s, b_b_s, ab_w_out, ab_norm_post, c_norm_pre, c_w_pw1, c_b_pw1, c_w_dw, c_b_dw, c_ln_g, c_ln_b, c_w_pw2, c_b_pw2, c_norm_post, f_norm_pre, f_w_gate, f_w_up, f_w_down, f_norm_post):
    depth = f_norm_pre.shape[0]
    for layer in range(depth):
        i = layer // 2
        if layer % 2 == 0:
            x = _ab_layer(x, ab_norm_pre[i], ab_w_in[i], a_w_group[i], a_scale[i], b_ln_g[i], b_ln_b[i],
                          b_w_s[i], b_b_s[i], ab_w_out[i], ab_norm_post[i], tile=AB_TILE)
        else:
            x = _conv_layer(x, c_norm_pre[i], c_w_pw1[i], c_b_pw1[i], c_w_dw[i], c_b_dw[i], c_ln_g[i],
                            c_ln_b[i], c_w_pw2[i], c_b_pw2[i], c_norm_post[i], tile=CONV_TILE)
        x = _ffn_layer(x, f_norm_pre[layer], f_w_gate[layer], f_w_up[layer], f_w_down[layer],
                       f_norm_post[layer], tile=FFN_TILE, ff_tile=FFN_FF_TILE)
    return x
```

```python
import functools

import jax
import jax.numpy as jnp
from jax import lax
from jax.experimental import pallas as pl
from jax.experimental.pallas import tpu as pltpu

F32 = jnp.float32
BF16 = jnp.bfloat16

EPS = 1e-6
POOL_WINDOWS = (2, 4, 8, 16)
SGU_HEADS = 4
SGU_CHUNK = 128
CONV_K = 31

V7X_VMEM_BYTES = 64 * 1024 * 1024
VMEM_LIMIT_BYTES = V7X_VMEM_BYTES - 3 * 1024 * 1024
LANES = 128
MXU_COLS = 256

POOL_HALO = 16
CONV_HALO = 32


def _rms_norm(xf, g):
    y = xf * lax.rsqrt(jnp.mean(xf * xf, axis=-1, keepdims=True) + EPS)
    return y * g


def _layer_norm(xf, g, b):
    mu = jnp.mean(xf, axis=-1, keepdims=True)
    var = jnp.mean(jnp.square(xf - mu), axis=-1, keepdims=True)
    return (xf - mu) * lax.rsqrt(var + EPS) * g + b


def _dot(a, b):
    return jnp.dot(a, b, preferred_element_type=F32)


PACK_ROWS = 256


def _pack_body(w_ref, o_ref):
    o_ref[...] = pltpu.bitcast(w_ref[...].astype(BF16), jnp.uint32)


def _pack_weight(w_stack, layer):
    _, k, n = w_stack.shape
    assert k % PACK_ROWS == 0 and n % LANES == 0
    return pl.pallas_call(
        _pack_body,
        grid=(k // PACK_ROWS,),
        in_specs=[pl.BlockSpec((None, PACK_ROWS, n), lambda i: (layer, i, 0))],
        out_specs=pl.BlockSpec((PACK_ROWS // 2, n), lambda i: (i, 0)),
        out_shape=jax.ShapeDtypeStruct((k // 2, n), jnp.uint32),
        compiler_params=pltpu.CompilerParams(dimension_semantics=("arbitrary",)),
        name="pack_weight",
    )(w_stack)


def _weight(w_ref, cols=slice(None)):
    return pltpu.bitcast(w_ref[:, cols], BF16)


def _carry_halo(scr_ref, halo, tile, first_tile):
    @pl.when(first_tile)
    def _():
        scr_ref[:, 0:halo, :] = jnp.zeros((scr_ref.shape[0], halo, LANES), scr_ref.dtype)

    @pl.when(jnp.logical_not(first_tile))
    def _():
        scr_ref[:, 0:halo, :] = scr_ref[:, tile:tile + halo, :]


def _ab_body(x_ref, npre_ref, w_in_ref, wgrp_ref, ascale_ref, lng_ref, lnb_ref, ws_ref, bs_ref,
             w_out_ref, npost_ref, o_ref, a_scr, mix_scr, *, tile, d_a, d_b):
    i = pl.program_id(1)
    g_a = d_a // len(POOL_WINDOWS)
    dh_b = d_b // SGU_HEADS

    xf = x_ref[...]
    h = _rms_norm(xf, npre_ref[...]).astype(BF16)

    _carry_halo(a_scr, POOL_HALO, tile, i == 0)
    a = _dot(h, _weight(w_in_ref, slice(0, d_a)))
    for s in range(d_a // LANES):
        a_scr[s, POOL_HALO:POOL_HALO + tile, :] = a[:, s * LANES:(s + 1) * LANES]
    pos = i * tile + lax.broadcasted_iota(jnp.int32, (tile, LANES), 0)
    for g, w in enumerate(POOL_WINDOWS):
        cols = slice(g * g_a, (g + 1) * g_a)
        count = jnp.minimum(pos + 1, w).astype(F32)
        d = []
        for s in range(g * g_a // LANES, (g + 1) * g_a // LANES):
            ag = a_scr[s, POOL_HALO:POOL_HALO + tile, :]
            win = ag
            for k in range(1, w):
                win = win + a_scr[s, POOL_HALO - k:POOL_HALO - k + tile, :]
            d.append((win / count - ag).astype(BF16))
        og = _dot(jnp.concatenate(d, axis=-1), wgrp_ref[g])
        mix_scr[:, cols] = (og * ascale_ref[:, cols]).astype(BF16)

    z_u = jax.nn.gelu(_dot(h, _weight(w_in_ref, slice(d_a, d_a + d_b))))
    z_v = jax.nn.gelu(_dot(h, _weight(w_in_ref, slice(d_a + d_b, d_a + 2 * d_b))))
    v = _layer_norm(z_v, lng_ref[...], lnb_ref[...]).astype(BF16)
    row = lax.broadcasted_iota(jnp.int32, (SGU_CHUNK, SGU_CHUNK), 0)
    col = lax.broadcasted_iota(jnp.int32, (SGU_CHUNK, SGU_CHUNK), 1)
    for hd in range(SGU_HEADS):
        w_tril = jnp.where(col <= row, ws_ref[hd], 0.0).astype(BF16)
        bias = bs_ref[:, hd:hd + 1]
        hcols = slice(hd * dh_b, (hd + 1) * dh_b)
        for c in range(tile // SGU_CHUNK):
            rows = slice(c * SGU_CHUNK, (c + 1) * SGU_CHUNK)
            mixed = _dot(w_tril, v[rows, hcols]) + bias
            mix_scr[rows, d_a + hd * dh_b:d_a + (hd + 1) * dh_b] = (z_u[rows, hcols] * mixed).astype(BF16)

    y = _dot(mix_scr[...], _weight(w_out_ref))
    o_ref[...] = xf + _rms_norm(y, npost_ref[...])


def _single_buffered(shape):
    return pl.BlockSpec(shape, lambda *_: (0,) * len(shape), pipeline_mode=pl.Buffered(1))


def _ab_layer(x, npre, w_in, wgrp, ascale, lng, lnb, ws, bs, w_out, npost, *, tile):
    bsz, seq, d = x.shape
    d_a = ascale.shape[-1]
    d_b = lng.shape[-1]
    assert seq % tile == 0 and tile % SGU_CHUNK == 0 and tile >= POOL_HALO
    assert (d_a // len(POOL_WINDOWS)) % LANES == 0
    act_spec = pl.BlockSpec((None, tile, d), lambda b, i: (b, i, 0))
    operands = (npre.reshape(1, d), w_in, wgrp.astype(BF16), ascale.reshape(1, d_a),
                lng.reshape(1, d_b), lnb.reshape(1, d_b), ws, bs.T, w_out, npost.reshape(1, d))
    return pl.pallas_call(
        functools.partial(_ab_body, tile=tile, d_a=d_a, d_b=d_b),
        grid=(bsz, seq // tile),
        in_specs=[act_spec] + [_single_buffered(op.shape) for op in operands],
        out_specs=act_spec,
        out_shape=jax.ShapeDtypeStruct(x.shape, F32),
        scratch_shapes=[pltpu.VMEM((d_a // LANES, tile + POOL_HALO, LANES), F32),
                        pltpu.VMEM((tile, d_a + d_b), BF16)],
        compiler_params=pltpu.CompilerParams(
            dimension_semantics=("arbitrary", "arbitrary"), vmem_limit_bytes=VMEM_LIMIT_BYTES),
        name="ab_layer",
    )(x, *operands)


CONV_ROW_BLOCK = 128


def _conv_body(x_ref, npre_ref, w1_ref, b1_ref, wdw_ref, bdw_ref, lng_ref, lnb_ref, w2_ref, b2_ref,
               npost_ref, o_ref, g_scr, c_scr, *, tile, d_c):
    i = pl.program_id(1)
    xf = x_ref[...]
    h = _rms_norm(xf, npre_ref[...]).astype(BF16)

    _carry_halo(g_scr, CONV_HALO, tile, i == 0)
    base = CONV_HALO - (CONV_K - 1)
    for cb in range(d_c // MXU_COLS):
        lin_cols = slice(cb * MXU_COLS, (cb + 1) * MXU_COLS)
        gate_cols = slice(d_c + cb * MXU_COLS, d_c + (cb + 1) * MXU_COLS)
        lin = _dot(h, _weight(w1_ref, lin_cols)) + b1_ref[:, lin_cols]
        gate = _dot(h, _weight(w1_ref, gate_cols)) + b1_ref[:, gate_cols]
        glu = lin * jax.nn.sigmoid(gate)
        for half in range(MXU_COLS // LANES):
            s = cb * (MXU_COLS // LANES) + half
            cols = slice(s * LANES, (s + 1) * LANES)
            g_scr[s, CONV_HALO:CONV_HALO + tile, :] = glu[:, half * LANES:(half + 1) * LANES]
            for r0 in range(0, tile, CONV_ROW_BLOCK):
                acc = wdw_ref[0:1, cols] * g_scr[s, base + r0:base + r0 + CONV_ROW_BLOCK, :]
                for k in range(1, CONV_K):
                    acc = acc + wdw_ref[k:k + 1, cols] * g_scr[s, base + k + r0:base + k + r0 + CONV_ROW_BLOCK, :]
                c_scr[r0:r0 + CONV_ROW_BLOCK, cols] = acc + bdw_ref[:, cols]

    c = jax.nn.silu(_layer_norm(c_scr[...], lng_ref[...], lnb_ref[...])).astype(BF16)
    y = _dot(c, _weight(w2_ref)) + b2_ref[...]
    o_ref[...] = xf + _rms_norm(y, npost_ref[...])


def _conv_layer(x, npre, w1, b1, wdw, bdw, lng, lnb, w2, b2, npost, *, tile):
    bsz, seq, d = x.shape
    d_c = wdw.shape[-1]
    assert seq % tile == 0 and tile % CONV_ROW_BLOCK == 0 and tile >= CONV_HALO
    assert d_c % MXU_COLS == 0 and wdw.shape[0] == CONV_K
    act_spec = pl.BlockSpec((None, tile, d), lambda b, i: (b, i, 0))
    operands = (npre.reshape(1, d), w1, b1.reshape(1, 2 * d_c), wdw, bdw.reshape(1, d_c),
                lng.reshape(1, d_c), lnb.reshape(1, d_c), w2, b2.reshape(1, d),
                npost.reshape(1, d))
    return pl.pallas_call(
        functools.partial(_conv_body, tile=tile, d_c=d_c),
        grid=(bsz, seq // tile),
        in_specs=[act_spec] + [_single_buffered(op.shape) for op in operands],
        out_specs=act_spec,
        out_shape=jax.ShapeDtypeStruct(x.shape, F32),
        scratch_shapes=[pltpu.VMEM((d_c // LANES, tile + CONV_HALO, LANES), F32),
                        pltpu.VMEM((tile, d_c), F32)],
        compiler_params=pltpu.CompilerParams(
            dimension_semantics=("arbitrary", "arbitrary"), vmem_limit_bytes=VMEM_LIMIT_BYTES),
        name="conv_layer",
    )(x, *operands)


FFN_ROW_CHUNK = 256


def _ffn_body(x_ref, npre_ref, wg_ref, wu_ref, wd_ref, npost_ref, o_ref, h_scr):
    j = pl.program_id(1)
    last = pl.num_programs(1) - 1
    tile = o_ref.shape[0]

    def swiglu(rows):
        h = h_scr[rows, :]
        a = jax.nn.silu(_dot(h, _weight(wg_ref))) * _dot(h, _weight(wu_ref))
        return _dot(a.astype(BF16), _weight(wd_ref))

    @pl.when(j == 0)
    def _():
        for r0 in range(0, tile, FFN_ROW_CHUNK):
            rows = slice(r0, r0 + FFN_ROW_CHUNK)
            h_scr[rows, :] = _rms_norm(x_ref[rows, :], npre_ref[...]).astype(BF16)
            o_ref[rows, :] = swiglu(rows)

    @pl.when(jnp.logical_and(j > 0, j < last))
    def _():
        o_ref[...] += swiglu(slice(None))

    @pl.when(j == last)
    def _():
        for r0 in range(0, tile, FFN_ROW_CHUNK):
            rows = slice(r0, r0 + FFN_ROW_CHUNK)
            y = o_ref[rows, :] + swiglu(rows)
            o_ref[rows, :] = x_ref[rows, :] + _rms_norm(y, npost_ref[...])


def _ffn_layer(x, npre, wg, wu, wd, npost, *, tile, ff_tile):
    bsz, seq, d = x.shape
    d_ff = wg.shape[-1]
    tokens = bsz * seq
    assert tokens % tile == 0 and tile % FFN_ROW_CHUNK == 0
    assert d_ff % ff_tile == 0 and d_ff // ff_tile >= 2
    act_spec = pl.BlockSpec((tile, d), lambda i, j: (i, 0))
    vec_spec = pl.BlockSpec((1, d), lambda i, j: (0, 0))
    out = pl.pallas_call(
        _ffn_body,
        grid=(tokens // tile, d_ff // ff_tile),
        in_specs=[act_spec, vec_spec,
                  pl.BlockSpec((d // 2, ff_tile), lambda i, j: (0, j)),
                  pl.BlockSpec((d // 2, ff_tile), lambda i, j: (0, j)),
                  pl.BlockSpec((ff_tile // 2, d), lambda i, j: (j, 0)),
                  vec_spec],
        out_specs=act_spec,
        out_shape=jax.ShapeDtypeStruct((tokens, d), F32),
        scratch_shapes=[pltpu.VMEM((tile, d), BF16)],
        compiler_params=pltpu.CompilerParams(
            dimension_semantics=("arbitrary", "arbitrary"), vmem_limit_bytes=VMEM_LIMIT_BYTES),
        name="ffn_layer",
    )(x.reshape(tokens, d), npre.reshape(1, d), wg, wu, wd, npost.reshape(1, d))
    return out.reshape(bsz, seq, d)


AB_TILE = 256
CONV_TILE = 256
FFN_TILE = 1024
FFN_FF_TILE = 512


def kernel(x, ab_norm_pre, ab_w_in, a_w_group, a_scale, b_ln_g, b_ln_b, b_w_s, b_b_s, ab_w_out, ab_norm_post, c_norm_pre, c_w_pw1, c_b_pw1, c_w_dw, c_b_dw, c_ln_g, c_ln_b, c_w_pw2, c_b_pw2, c_norm_post, f_norm_pre, f_w_gate, f_w_up, f_w_down, f_norm_post):
    depth = f_norm_pre.shape[0]
    for layer in range(depth):
        i = layer // 2
        if layer % 2 == 0:
            x = _ab_layer(x, ab_norm_pre[i], _pack_weight(ab_w_in, i), a_w_group[i], a_scale[i], b_ln_g[i],
                          b_ln_b[i], b_w_s[i], b_b_s[i], _pack_weight(ab_w_out, i), ab_norm_post[i],
                          tile=AB_TILE)
        else:
            x = _conv_layer(x, c_norm_pre[i], _pack_weight(c_w_pw1, i), c_b_pw1[i], c_w_dw[i], c_b_dw[i],
                            c_ln_g[i], c_ln_b[i], _pack_weight(c_w_pw2, i), c_b_pw2[i], c_norm_post[i],
                            tile=CONV_TILE)
        x = _ffn_layer(x, f_norm_pre[layer], _pack_weight(f_w_gate, layer), _pack_weight(f_w_up, layer),
                       _pack_weight(f_w_down, layer), f_norm_post[layer], tile=FFN_TILE, ff_tile=FFN_FF_TILE)
    return x
```

```python
import functools

import jax
import jax.numpy as jnp
from jax import lax
from jax.experimental import pallas as pl
from jax.experimental.pallas import tpu as pltpu

F32 = jnp.float32
BF16 = jnp.bfloat16

EPS = 1e-6
POOL_WINDOWS = (2, 4, 8, 16)
SGU_HEADS = 4
SGU_CHUNK = 128
CONV_K = 31

V7X_VMEM_BYTES = 64 * 1024 * 1024
VMEM_LIMIT_BYTES = V7X_VMEM_BYTES - 3 * 1024 * 1024
LANES = 128
MXU_COLS = 256

POOL_HALO = 16
CONV_HALO = 32


def _rms_norm(xf, g):
    y = xf * lax.rsqrt(jnp.mean(xf * xf, axis=-1, keepdims=True) + EPS)
    return y * g


def _layer_norm(xf, g, b):
    mu = jnp.mean(xf, axis=-1, keepdims=True)
    var = jnp.mean(jnp.square(xf - mu), axis=-1, keepdims=True)
    return (xf - mu) * lax.rsqrt(var + EPS) * g + b


def _dot(a, b):
    return jnp.dot(a, b, preferred_element_type=F32)


PACK_ROWS = 256


def _pack_body(w_ref, o_ref):
    o_ref[...] = pltpu.bitcast(w_ref[...].astype(BF16), jnp.uint32)


def _pack_weight(w_stack, layer):
    _, k, n = w_stack.shape
    assert k % PACK_ROWS == 0 and n % LANES == 0
    return pl.pallas_call(
        _pack_body,
        grid=(k // PACK_ROWS,),
        in_specs=[pl.BlockSpec((None, PACK_ROWS, n), lambda i: (layer, i, 0))],
        out_specs=pl.BlockSpec((PACK_ROWS // 2, n), lambda i: (i, 0)),
        out_shape=jax.ShapeDtypeStruct((k // 2, n), jnp.uint32),
        compiler_params=pltpu.CompilerParams(dimension_semantics=("arbitrary",)),
        name="pack_weight",
    )(w_stack)


def _weight(w_ref, cols=slice(None)):
    return pltpu.bitcast(w_ref[:, cols], BF16)


def _carry_halo(scr_ref, halo, tile, first_tile):
    @pl.when(first_tile)
    def _():
        scr_ref[:, 0:halo, :] = jnp.zeros((scr_ref.shape[0], halo, LANES), scr_ref.dtype)

    @pl.when(jnp.logical_not(first_tile))
    def _():
        scr_ref[:, 0:halo, :] = scr_ref[:, tile:tile + halo, :]


AB_SUB_TILE = 256


def _ab_body(x_ref, npre_ref, w_in_ref, wgrp_ref, ascale_ref, lng_ref, lnb_ref, ws_ref, bs_ref,
             w_out_ref, npost_ref, o_ref, a_scr, mix_scr, *, tile, d_a, d_b):
    i = pl.program_id(1)
    g_a = d_a // len(POOL_WINDOWS)
    dh_b = d_b // SGU_HEADS

    _carry_halo(a_scr, POOL_HALO, tile, i == 0)
    row = lax.broadcasted_iota(jnp.int32, (SGU_CHUNK, SGU_CHUNK), 0)
    col = lax.broadcasted_iota(jnp.int32, (SGU_CHUNK, SGU_CHUNK), 1)
    for t0 in range(0, tile, AB_SUB_TILE):
        trows = slice(t0, t0 + AB_SUB_TILE)
        xf = x_ref[trows, :]
        h = _rms_norm(xf, npre_ref[...]).astype(BF16)

        z_u = jax.nn.gelu(_dot(h, _weight(w_in_ref, slice(d_a, d_a + d_b))))
        z_v = jax.nn.gelu(_dot(h, _weight(w_in_ref, slice(d_a + d_b, d_a + 2 * d_b))))
        v = _layer_norm(z_v, lng_ref[...], lnb_ref[...]).astype(BF16)
        for hd in range(SGU_HEADS):
            w_tril = jnp.where(col <= row, ws_ref[hd], 0.0).astype(BF16)
            bias = bs_ref[:, hd:hd + 1]
            hcols = slice(hd * dh_b, (hd + 1) * dh_b)
            for c in range(AB_SUB_TILE // SGU_CHUNK):
                rows = slice(c * SGU_CHUNK, (c + 1) * SGU_CHUNK)
                mixed = _dot(w_tril, v[rows, hcols]) + bias
                mix_scr[t0 + c * SGU_CHUNK:t0 + (c + 1) * SGU_CHUNK, d_a + hd * dh_b:d_a + (hd + 1) * dh_b] = (
                    z_u[rows, hcols] * mixed).astype(BF16)

        a = _dot(h, _weight(w_in_ref, slice(0, d_a)))
        lo = POOL_HALO + t0
        for s in range(d_a // LANES):
            a_scr[s, lo:lo + AB_SUB_TILE, :] = a[:, s * LANES:(s + 1) * LANES]
        pos = i * tile + t0 + lax.broadcasted_iota(jnp.int32, (AB_SUB_TILE, LANES), 0)
        for g, w in enumerate(POOL_WINDOWS):
            cols = slice(g * g_a, (g + 1) * g_a)
            count = jnp.minimum(pos + 1, w).astype(F32)
            d = []
            for s in range(g * g_a // LANES, (g + 1) * g_a // LANES):
                ag = a_scr[s, lo:lo + AB_SUB_TILE, :]
                win = ag
                for k in range(1, w):
                    win = win + a_scr[s, lo - k:lo - k + AB_SUB_TILE, :]
                d.append((win / count - ag).astype(BF16))
            og = _dot(jnp.concatenate(d, axis=-1), wgrp_ref[g])
            mix_scr[trows, cols] = (og * ascale_ref[:, cols]).astype(BF16)

        y = _dot(mix_scr[trows, :], _weight(w_out_ref))
        o_ref[trows, :] = xf + _rms_norm(y, npost_ref[...])


def _single_buffered(shape):
    return pl.BlockSpec(shape, lambda *_: (0,) * len(shape), pipeline_mode=pl.Buffered(1))


def _ab_layer(x, npre, w_in, wgrp, ascale, lng, lnb, ws, bs, w_out, npost, *, tile):
    bsz, seq, d = x.shape
    d_a = ascale.shape[-1]
    d_b = lng.shape[-1]
    assert seq % tile == 0 and tile % AB_SUB_TILE == 0 and AB_SUB_TILE % SGU_CHUNK == 0
    assert (d_a // len(POOL_WINDOWS)) % LANES == 0 and AB_SUB_TILE >= POOL_HALO
    act_spec = pl.BlockSpec((None, tile, d), lambda b, i: (b, i, 0))
    operands = (npre.reshape(1, d), w_in, wgrp.astype(BF16), ascale.reshape(1, d_a),
                lng.reshape(1, d_b), lnb.reshape(1, d_b), ws, bs.T, w_out, npost.reshape(1, d))
    return pl.pallas_call(
        functools.partial(_ab_body, tile=tile, d_a=d_a, d_b=d_b),
        grid=(bsz, seq // tile),
        in_specs=[act_spec] + [_single_buffered(op.shape) for op in operands],
        out_specs=act_spec,
        out_shape=jax.ShapeDtypeStruct(x.shape, F32),
        scratch_shapes=[pltpu.VMEM((d_a // LANES, tile + POOL_HALO, LANES), F32),
                        pltpu.VMEM((tile, d_a + d_b), BF16)],
        compiler_params=pltpu.CompilerParams(
            dimension_semantics=("arbitrary", "arbitrary"), vmem_limit_bytes=VMEM_LIMIT_BYTES),
        name="ab_layer",
    )(x, *operands)


CONV_ROW_BLOCK = 128


def _conv_body(x_ref, npre_ref, w1_ref, b1_ref, wdw_ref, bdw_ref, lng_ref, lnb_ref, w2_ref, b2_ref,
               npost_ref, o_ref, g_scr, c_scr, *, tile, d_c):
    i = pl.program_id(1)
    xf = x_ref[...]
    h = _rms_norm(xf, npre_ref[...]).astype(BF16)

    _carry_halo(g_scr, CONV_HALO, tile, i == 0)
    base = CONV_HALO - (CONV_K - 1)
    for cb in range(d_c // MXU_COLS):
        lin_cols = slice(cb * MXU_COLS, (cb + 1) * MXU_COLS)
        gate_cols = slice(d_c + cb * MXU_COLS, d_c + (cb + 1) * MXU_COLS)
        lin = _dot(h, _weight(w1_ref, lin_cols)) + b1_ref[:, lin_cols]
        gate = _dot(h, _weight(w1_ref, gate_cols)) + b1_ref[:, gate_cols]
        glu = lin * jax.nn.sigmoid(gate)
        for half in range(MXU_COLS // LANES):
            s = cb * (MXU_COLS // LANES) + half
            cols = slice(s * LANES, (s + 1) * LANES)
            g_scr[s, CONV_HALO:CONV_HALO + tile, :] = glu[:, half * LANES:(half + 1) * LANES]
            for r0 in range(0, tile, CONV_ROW_BLOCK):
                acc = wdw_ref[0:1, cols] * g_scr[s, base + r0:base + r0 + CONV_ROW_BLOCK, :]
                for k in range(1, CONV_K):
                    acc = acc + wdw_ref[k:k + 1, cols] * g_scr[s, base + k + r0:base + k + r0 + CONV_ROW_BLOCK, :]
                c_scr[r0:r0 + CONV_ROW_BLOCK, cols] = acc + bdw_ref[:, cols]

    c = jax.nn.silu(_layer_norm(c_scr[...], lng_ref[...], lnb_ref[...])).astype(BF16)
    y = _dot(c, _weight(w2_ref)) + b2_ref[...]
    o_ref[...] = xf + _rms_norm(y, npost_ref[...])


def _conv_layer(x, npre, w1, b1, wdw, bdw, lng, lnb, w2, b2, npost, *, tile):
    bsz, seq, d = x.shape
    d_c = wdw.shape[-1]
    assert seq % tile == 0 and tile % CONV_ROW_BLOCK == 0 and tile >= CONV_HALO
    assert d_c % MXU_COLS == 0 and wdw.shape[0] == CONV_K
    act_spec = pl.BlockSpec((None, tile, d), lambda b, i: (b, i, 0))
    operands = (npre.reshape(1, d), w1, b1.reshape(1, 2 * d_c), wdw, bdw.reshape(1, d_c),
                lng.reshape(1, d_c), lnb.reshape(1, d_c), w2, b2.reshape(1, d),
                npost.reshape(1, d))
    return pl.pallas_call(
        functools.partial(_conv_body, tile=tile, d_c=d_c),
        grid=(bsz, seq // tile),
        in_specs=[act_spec] + [_single_buffered(op.shape) for op in operands],
        out_specs=act_spec,
        out_shape=jax.ShapeDtypeStruct(x.shape, F32),
        scratch_shapes=[pltpu.VMEM((d_c // LANES, tile + CONV_HALO, LANES), F32),
                        pltpu.VMEM((tile, d_c), F32)],
        compiler_params=pltpu.CompilerParams(
            dimension_semantics=("arbitrary", "arbitrary"), vmem_limit_bytes=VMEM_LIMIT_BYTES),
        name="conv_layer",
    )(x, *operands)


FFN_ROW_CHUNK = 256


def _ffn_body(x_ref, npre_ref, wg_ref, wu_ref, wd_ref, npost_ref, o_ref, h_scr):
    j = pl.program_id(1)
    last = pl.num_programs(1) - 1
    tile = o_ref.shape[0]
    ff_tile = wg_ref.shape[1]

    def swiglu(rows, col_block):
        h = h_scr[rows, :]
        out = None
        for c0 in range(0, wg_ref.shape[1], col_block):
            cols = slice(c0, c0 + col_block)
            a = jax.nn.silu(_dot(h, _weight(wg_ref, cols))) * _dot(h, _weight(wu_ref, cols))
            part = _dot(a.astype(BF16), pltpu.bitcast(wd_ref[c0 // 2:(c0 + col_block) // 2, :], BF16))
            out = part if out is None else out + part
        return out

    @pl.when(j == 0)
    def _():
        for r0 in range(0, tile, FFN_ROW_CHUNK):
            rows = slice(r0, r0 + FFN_ROW_CHUNK)
            h_scr[rows, :] = _rms_norm(x_ref[rows, :], npre_ref[...]).astype(BF16)
            o_ref[rows, :] = swiglu(rows, ff_tile)

    @pl.when(jnp.logical_and(j > 0, j < last))
    def _():
        o_ref[...] += swiglu(slice(None), MXU_COLS)

    @pl.when(j == last)
    def _():
        for r0 in range(0, tile, FFN_ROW_CHUNK):
            rows = slice(r0, r0 + FFN_ROW_CHUNK)
            y = o_ref[rows, :] + swiglu(rows, ff_tile)
            o_ref[rows, :] = x_ref[rows, :] + _rms_norm(y, npost_ref[...])


def _ffn_layer(x, npre, wg, wu, wd, npost, *, tile, ff_tile):
    bsz, seq, d = x.shape
    d_ff = wg.shape[-1]
    tokens = bsz * seq
    assert tokens % tile == 0 and tile % FFN_ROW_CHUNK == 0
    assert d_ff % ff_tile == 0 and d_ff // ff_tile >= 2
    act_spec = pl.BlockSpec((tile, d), lambda i, j: (i, 0))
    vec_spec = pl.BlockSpec((1, d), lambda i, j: (0, 0))
    out = pl.pallas_call(
        _ffn_body,
        grid=(tokens // tile, d_ff // ff_tile),
        in_specs=[act_spec, vec_spec,
                  pl.BlockSpec((d // 2, ff_tile), lambda i, j: (0, j)),
                  pl.BlockSpec((d // 2, ff_tile), lambda i, j: (0, j)),
                  pl.BlockSpec((ff_tile // 2, d), lambda i, j: (j, 0)),
                  vec_spec],
        out_specs=act_spec,
        out_shape=jax.ShapeDtypeStruct((tokens, d), F32),
        scratch_shapes=[pltpu.VMEM((tile, d), BF16)],
        compiler_params=pltpu.CompilerParams(
            dimension_semantics=("arbitrary", "arbitrary"), vmem_limit_bytes=VMEM_LIMIT_BYTES),
        name="ffn_layer",
    )(x.reshape(tokens, d), npre.reshape(1, d), wg, wu, wd, npost.reshape(1, d))
    return out.reshape(bsz, seq, d)


AB_TILE = 512
CONV_TILE = 256
FFN_TILE = 1024
FFN_FF_TILE = 512


def kernel(x, ab_norm_pre, ab_w_in, a_w_group, a_scale, b_ln_g, b_ln_b, b_w_s, b_b_s, ab_w_out, ab_norm_post, c_norm_pre, c_w_pw1, c_b_pw1, c_w_dw, c_b_dw, c_ln_g, c_ln_b, c_w_pw2, c_b_pw2, c_norm_post, f_norm_pre, f_w_gate, f_w_up, f_w_down, f_norm_post):
    depth = f_norm_pre.shape[0]
    for layer in range(depth):
        i = layer // 2
        if layer % 2 == 0:
            x = _ab_layer(x, ab_norm_pre[i], _pack_weight(ab_w_in, i), a_w_group[i], a_scale[i], b_ln_g[i],
                          b_ln_b[i], b_w_s[i], b_b_s[i], _pack_weight(ab_w_out, i), ab_norm_post[i],
                          tile=AB_TILE)
        else:
            x = _conv_layer(x, c_norm_pre[i], _pack_weight(c_w_pw1, i), c_b_pw1[i], c_w_dw[i], c_b_dw[i],
                            c_ln_g[i], c_ln_b[i], _pack_weight(c_w_pw2, i), c_b_pw2[i], c_norm_post[i],
                            tile=CONV_TILE)
        x = _ffn_layer(x, f_norm_pre[layer], _pack_weight(f_w_gate, layer), _pack_weight(f_w_up, layer),
                       _pack_weight(f_w_down, layer), f_norm_post[layer], tile=FFN_TILE, ff_tile=FFN_FF_TILE)
    return x
```

```python
import functools

import jax
import jax.numpy as jnp
from jax import lax
from jax.experimental import pallas as pl
from jax.experimental.pallas import tpu as pltpu

F32 = jnp.float32
BF16 = jnp.bfloat16

EPS = 1e-6
POOL_WINDOWS = (2, 4, 8, 16)
SGU_HEADS = 4
SGU_CHUNK = 128
CONV_K = 31

V7X_VMEM_BYTES = 64 * 1024 * 1024
VMEM_LIMIT_BYTES = V7X_VMEM_BYTES - 3 * 1024 * 1024
LANES = 128
MXU_COLS = 256

POOL_HALO = 16
CONV_HALO = 32


def _rms_norm(xf, g):
    y = xf * lax.rsqrt(jnp.mean(xf * xf, axis=-1, keepdims=True) + EPS)
    return y * g


def _layer_norm(xf, g, b):
    mu = jnp.mean(xf, axis=-1, keepdims=True)
    var = jnp.mean(jnp.square(xf - mu), axis=-1, keepdims=True)
    return (xf - mu) * lax.rsqrt(var + EPS) * g + b


def _dot(a, b):
    return jnp.dot(a, b, preferred_element_type=F32)


PACK_ROWS = 256


def _pack_body(w_ref, o_ref):
    o_ref[...] = pltpu.bitcast(w_ref[...].astype(BF16), jnp.uint32)


def _pack_weight(w_stack, layer):
    _, k, n = w_stack.shape
    assert k % PACK_ROWS == 0 and n % LANES == 0
    return pl.pallas_call(
        _pack_body,
        grid=(k // PACK_ROWS,),
        in_specs=[pl.BlockSpec((None, PACK_ROWS, n), lambda i: (layer, i, 0))],
        out_specs=pl.BlockSpec((PACK_ROWS // 2, n), lambda i: (i, 0)),
        out_shape=jax.ShapeDtypeStruct((k // 2, n), jnp.uint32),
        compiler_params=pltpu.CompilerParams(dimension_semantics=("arbitrary",)),
        name="pack_weight",
    )(w_stack)


def _weight(w_ref, cols=slice(None)):
    return pltpu.bitcast(w_ref[:, cols], BF16)


def _carry_halo(scr_ref, halo, tile, first_tile):
    @pl.when(first_tile)
    def _():
        scr_ref[:, 0:halo, :] = jnp.zeros((scr_ref.shape[0], halo, LANES), scr_ref.dtype)

    @pl.when(jnp.logical_not(first_tile))
    def _():
        scr_ref[:, 0:halo, :] = scr_ref[:, tile:tile + halo, :]


AB_SUB_TILE = 256


def _ab_body(x_ref, npre_ref, w_in_ref, wgrp_ref, ascale_ref, lng_ref, lnb_ref, ws_ref, bs_ref,
             w_out_ref, npost_ref, o_ref, a_scr, mix_scr, *, tile, d_a, d_b):
    i = pl.program_id(1)
    g_a = d_a // len(POOL_WINDOWS)
    dh_b = d_b // SGU_HEADS

    _carry_halo(a_scr, POOL_HALO, tile, i == 0)
    row = lax.broadcasted_iota(jnp.int32, (SGU_CHUNK, SGU_CHUNK), 0)
    col = lax.broadcasted_iota(jnp.int32, (SGU_CHUNK, SGU_CHUNK), 1)
    for t0 in range(0, tile, AB_SUB_TILE):
        trows = slice(t0, t0 + AB_SUB_TILE)
        xf = x_ref[trows, :]
        h = _rms_norm(xf, npre_ref[...]).astype(BF16)

        z_v = jax.nn.gelu(_dot(h, _weight(w_in_ref, slice(d_a + d_b, d_a + 2 * d_b))))
        v = _layer_norm(z_v, lng_ref[...], lnb_ref[...]).astype(BF16)

        a = _dot(h, _weight(w_in_ref, slice(0, d_a)))
        lo = POOL_HALO + t0
        for s in range(d_a // LANES):
            a_scr[s, lo:lo + AB_SUB_TILE, :] = a[:, s * LANES:(s + 1) * LANES]
        pos = i * tile + t0 + lax.broadcasted_iota(jnp.int32, (AB_SUB_TILE, LANES), 0)
        for g, w in enumerate(POOL_WINDOWS):
            cols = slice(g * g_a, (g + 1) * g_a)
            count = jnp.minimum(pos + 1, w).astype(F32)
            d = []
            for s in range(g * g_a // LANES, (g + 1) * g_a // LANES):
                ag = a_scr[s, lo:lo + AB_SUB_TILE, :]
                win = ag
                for k in range(1, w):
                    win = win + a_scr[s, lo - k:lo - k + AB_SUB_TILE, :]
                d.append((win / count - ag).astype(BF16))
            og = _dot(jnp.concatenate(d, axis=-1), wgrp_ref[g])
            mix_scr[trows, cols] = (og * ascale_ref[:, cols]).astype(BF16)

        z_u = jax.nn.gelu(_dot(h, _weight(w_in_ref, slice(d_a, d_a + d_b))))
        for hd in range(SGU_HEADS):
            w_tril = jnp.where(col <= row, ws_ref[hd], 0.0).astype(BF16)
            bias = bs_ref[:, hd:hd + 1]
            hcols = slice(hd * dh_b, (hd + 1) * dh_b)
            for c in range(AB_SUB_TILE // SGU_CHUNK):
                rows = slice(c * SGU_CHUNK, (c + 1) * SGU_CHUNK)
                mixed = _dot(w_tril, v[rows, hcols]) + bias
                mix_scr[t0 + c * SGU_CHUNK:t0 + (c + 1) * SGU_CHUNK, d_a + hd * dh_b:d_a + (hd + 1) * dh_b] = (
                    z_u[rows, hcols] * mixed).astype(BF16)

        y = _dot(mix_scr[trows, :], _weight(w_out_ref))
        o_ref[trows, :] = xf + _rms_norm(y, npost_ref[...])


def _single_buffered(shape):
    return pl.BlockSpec(shape, lambda *_: (0,) * len(shape), pipeline_mode=pl.Buffered(1))


def _ab_layer(x, npre, w_in, wgrp, ascale, lng, lnb, ws, bs, w_out, npost, *, tile):
    bsz, seq, d = x.shape
    d_a = ascale.shape[-1]
    d_b = lng.shape[-1]
    assert seq % tile == 0 and tile % AB_SUB_TILE == 0 and AB_SUB_TILE % SGU_CHUNK == 0
    assert (d_a // len(POOL_WINDOWS)) % LANES == 0 and AB_SUB_TILE >= POOL_HALO
    act_spec = pl.BlockSpec((None, tile, d), lambda b, i: (b, i, 0))
    operands = (npre.reshape(1, d), w_in, wgrp.astype(BF16), ascale.reshape(1, d_a),
                lng.reshape(1, d_b), lnb.reshape(1, d_b), ws, bs.T, w_out, npost.reshape(1, d))
    return pl.pallas_call(
        functools.partial(_ab_body, tile=tile, d_a=d_a, d_b=d_b),
        grid=(bsz, seq // tile),
        in_specs=[act_spec] + [_single_buffered(op.shape) for op in operands],
        out_specs=act_spec,
        out_shape=jax.ShapeDtypeStruct(x.shape, F32),
        scratch_shapes=[pltpu.VMEM((d_a // LANES, tile + POOL_HALO, LANES), F32),
                        pltpu.VMEM((tile, d_a + d_b), BF16)],
        compiler_params=pltpu.CompilerParams(
            dimension_semantics=("arbitrary", "arbitrary"), vmem_limit_bytes=VMEM_LIMIT_BYTES),
        name="ab_layer",
    )(x, *operands)


CONV_ROW_BLOCK = 128


def _conv_body(x_ref, npre_ref, w1_ref, b1_ref, wdw_ref, bdw_ref, lng_ref, lnb_ref, w2_ref, b2_ref,
               npost_ref, o_ref, g_scr, c_scr, *, tile, d_c):
    i = pl.program_id(1)
    xf = x_ref[...]
    h = _rms_norm(xf, npre_ref[...]).astype(BF16)

    _carry_halo(g_scr, CONV_HALO, tile, i == 0)
    base = CONV_HALO - (CONV_K - 1)
    for cb in range(d_c // MXU_COLS):
        lin_cols = slice(cb * MXU_COLS, (cb + 1) * MXU_COLS)
        gate_cols = slice(d_c + cb * MXU_COLS, d_c + (cb + 1) * MXU_COLS)
        lin = _dot(h, _weight(w1_ref, lin_cols)) + b1_ref[:, lin_cols]
        gate = _dot(h, _weight(w1_ref, gate_cols)) + b1_ref[:, gate_cols]
        glu = lin * jax.nn.sigmoid(gate)
        for half in range(MXU_COLS // LANES):
            s = cb * (MXU_COLS // LANES) + half
            cols = slice(s * LANES, (s + 1) * LANES)
            g_scr[s, CONV_HALO:CONV_HALO + tile, :] = glu[:, half * LANES:(half + 1) * LANES]
            for r0 in range(0, tile, CONV_ROW_BLOCK):
                acc = wdw_ref[0:1, cols] * g_scr[s, base + r0:base + r0 + CONV_ROW_BLOCK, :]
                for k in range(1, CONV_K):
                    acc = acc + wdw_ref[k:k + 1, cols] * g_scr[s, base + k + r0:base + k + r0 + CONV_ROW_BLOCK, :]
                c_scr[r0:r0 + CONV_ROW_BLOCK, cols] = acc + bdw_ref[:, cols]

    c = jax.nn.silu(_layer_norm(c_scr[...], lng_ref[...], lnb_ref[...])).astype(BF16)
    y = _dot(c, _weight(w2_ref)) + b2_ref[...]
    o_ref[...] = xf + _rms_norm(y, npost_ref[...])


def _conv_layer(x, npre, w1, b1, wdw, bdw, lng, lnb, w2, b2, npost, *, tile):
    bsz, seq, d = x.shape
    d_c = wdw.shape[-1]
    assert seq % tile == 0 and tile % CONV_ROW_BLOCK == 0 and tile >= CONV_HALO
    assert d_c % MXU_COLS == 0 and wdw.shape[0] == CONV_K
    act_spec = pl.BlockSpec((None, tile, d), lambda b, i: (b, i, 0))
    operands = (npre.reshape(1, d), w1, b1.reshape(1, 2 * d_c), wdw, bdw.reshape(1, d_c),
                lng.reshape(1, d_c), lnb.reshape(1, d_c), w2, b2.reshape(1, d),
                npost.reshape(1, d))
    return pl.pallas_call(
        functools.partial(_conv_body, tile=tile, d_c=d_c),
        grid=(bsz, seq // tile),
        in_specs=[act_spec] + [_single_buffered(op.shape) for op in operands],
        out_specs=act_spec,
        out_shape=jax.ShapeDtypeStruct(x.shape, F32),
        scratch_shapes=[pltpu.VMEM((d_c // LANES, tile + CONV_HALO, LANES), F32),
                        pltpu.VMEM((tile, d_c), F32)],
        compiler_params=pltpu.CompilerParams(
            dimension_semantics=("arbitrary", "arbitrary"), vmem_limit_bytes=VMEM_LIMIT_BYTES),
        name="conv_layer",
    )(x, *operands)


FFN_ROW_CHUNK = 256


def _ffn_body(x_ref, npre_ref, wg_ref, wu_ref, wd_ref, npost_ref, o_ref, h_scr):
    j = pl.program_id(1)
    last = pl.num_programs(1) - 1
    tile = o_ref.shape[0]
    ff_tile = wg_ref.shape[1]

    def swiglu(rows, col_block):
        h = h_scr[rows, :]
        out = None
        for c0 in range(0, wg_ref.shape[1], col_block):
            cols = slice(c0, c0 + col_block)
            a = jax.nn.silu(_dot(h, _weight(wg_ref, cols))) * _dot(h, _weight(wu_ref, cols))
            part = _dot(a.astype(BF16), pltpu.bitcast(wd_ref[c0 // 2:(c0 + col_block) // 2, :], BF16))
            out = part if out is None else out + part
        return out

    @pl.when(j == 0)
    def _():
        for r0 in range(0, tile, FFN_ROW_CHUNK):
            rows = slice(r0, r0 + FFN_ROW_CHUNK)
            h_scr[rows, :] = _rms_norm(x_ref[rows, :], npre_ref[...]).astype(BF16)
            o_ref[rows, :] = swiglu(rows, ff_tile)

    @pl.when(jnp.logical_and(j > 0, j < last))
    def _():
        o_ref[...] += swiglu(slice(None), MXU_COLS)

    @pl.when(j == last)
    def _():
        for r0 in range(0, tile, FFN_ROW_CHUNK):
            rows = slice(r0, r0 + FFN_ROW_CHUNK)
            y = o_ref[rows, :] + swiglu(rows, ff_tile)
            o_ref[rows, :] = x_ref[rows, :] + _rms_norm(y, npost_ref[...])


def _ffn_layer(x, npre, wg, wu, wd, npost, *, tile, ff_tile):
    bsz, seq, d = x.shape
    d_ff = wg.shape[-1]
    tokens = bsz * seq
    assert tokens % tile == 0 and tile % FFN_ROW_CHUNK == 0
    assert d_ff % ff_tile == 0 and d_ff // ff_tile >= 2
    act_spec = pl.BlockSpec((tile, d), lambda i, j: (i, 0))
    vec_spec = pl.BlockSpec((1, d), lambda i, j: (0, 0))
    out = pl.pallas_call(
        _ffn_body,
        grid=(tokens // tile, d_ff // ff_tile),
        in_specs=[act_spec, vec_spec,
                  pl.BlockSpec((d // 2, ff_tile), lambda i, j: (0, j)),
                  pl.BlockSpec((d // 2, ff_tile), lambda i, j: (0, j)),
                  pl.BlockSpec((ff_tile // 2, d), lambda i, j: (j, 0)),
                  vec_spec],
        out_specs=act_spec,
        out_shape=jax.ShapeDtypeStruct((tokens, d), F32),
        scratch_shapes=[pltpu.VMEM((tile, d), BF16)],
        compiler_params=pltpu.CompilerParams(
            dimension_semantics=("arbitrary", "arbitrary"), vmem_limit_bytes=VMEM_LIMIT_BYTES),
        name="ffn_layer",
    )(x.reshape(tokens, d), npre.reshape(1, d), wg, wu, wd, npost.reshape(1, d))
    return out.reshape(bsz, seq, d)


AB_TILE = 512
CONV_TILE = 256
FFN_TILE = 1024
FFN_FF_TILE = 512


def kernel(x, ab_norm_pre, ab_w_in, a_w_group, a_scale, b_ln_g, b_ln_b, b_w_s, b_b_s, ab_w_out, ab_norm_post, c_norm_pre, c_w_pw1, c_b_pw1, c_w_dw, c_b_dw, c_ln_g, c_ln_b, c_w_pw2, c_b_pw2, c_norm_post, f_norm_pre, f_w_gate, f_w_up, f_w_down, f_norm_post):
    depth = f_norm_pre.shape[0]
    for layer in range(depth):
        i = layer // 2
        if layer % 2 == 0:
            x = _ab_layer(x, ab_norm_pre[i], _pack_weight(ab_w_in, i), a_w_group[i], a_scale[i], b_ln_g[i],
                          b_ln_b[i], b_w_s[i], b_b_s[i], _pack_weight(ab_w_out, i), ab_norm_post[i],
                          tile=AB_TILE)
        else:
            x = _conv_layer(x, c_norm_pre[i], _pack_weight(c_w_pw1, i), c_b_pw1[i], c_w_dw[i], c_b_dw[i],
                            c_ln_g[i], c_ln_b[i], _pack_weight(c_w_pw2, i), c_b_pw2[i], c_norm_post[i],
                            tile=CONV_TILE)
        x = _ffn_layer(x, f_norm_pre[layer], _pack_weight(f_w_gate, layer), _pack_weight(f_w_up, layer),
                       _pack_weight(f_w_down, layer), f_norm_post[layer], tile=FFN_TILE, ff_tile=FFN_FF_TILE)
    return x
```

```python
import functools

import jax
import jax.numpy as jnp
from jax import lax
from jax.experimental import pallas as pl
from jax.experimental.pallas import tpu as pltpu

F32 = jnp.float32
BF16 = jnp.bfloat16

EPS = 1e-6
POOL_WINDOWS = (2, 4, 8, 16)
SGU_HEADS = 4
SGU_CHUNK = 128
CONV_K = 31

V7X_VMEM_BYTES = 64 * 1024 * 1024
VMEM_LIMIT_BYTES = V7X_VMEM_BYTES - 3 * 1024 * 1024
LANES = 128
MXU_COLS = 256

POOL_HALO = 16
CONV_HALO = 32


def _rms_norm(xf, g):
    y = xf * lax.rsqrt(jnp.mean(xf * xf, axis=-1, keepdims=True) + EPS)
    return y * g


def _layer_norm(xf, g, b):
    mu = jnp.mean(xf, axis=-1, keepdims=True)
    var = jnp.mean(jnp.square(xf - mu), axis=-1, keepdims=True)
    return (xf - mu) * lax.rsqrt(var + EPS) * g + b


def _dot(a, b):
    return jnp.dot(a, b, preferred_element_type=F32)


PACK_ROWS = 256


def _pack_body(w_ref, o_ref):
    o_ref[...] = pltpu.bitcast(w_ref[...].astype(BF16), jnp.uint32)


def _pack_weight(w_stack, layer):
    _, k, n = w_stack.shape
    assert k % PACK_ROWS == 0 and n % LANES == 0
    return pl.pallas_call(
        _pack_body,
        grid=(k // PACK_ROWS,),
        in_specs=[pl.BlockSpec((None, PACK_ROWS, n), lambda i: (layer, i, 0))],
        out_specs=pl.BlockSpec((PACK_ROWS // 2, n), lambda i: (i, 0)),
        out_shape=jax.ShapeDtypeStruct((k // 2, n), jnp.uint32),
        compiler_params=pltpu.CompilerParams(dimension_semantics=("arbitrary",)),
        name="pack_weight",
    )(w_stack)


def _pack_pair_body(a_ref, b_ref, oa_ref, ob_ref):
    oa_ref[...] = pltpu.bitcast(a_ref[...].astype(BF16), jnp.uint32)
    ob_ref[...] = pltpu.bitcast(b_ref[...].astype(BF16), jnp.uint32)


def _pack_weight_pair(a_stack, b_stack, layer):
    _, k, n = a_stack.shape
    assert a_stack.shape == b_stack.shape and k % PACK_ROWS == 0 and n % LANES == 0
    in_spec = pl.BlockSpec((None, PACK_ROWS, n), lambda i: (layer, i, 0))
    out_spec = pl.BlockSpec((PACK_ROWS // 2, n), lambda i: (i, 0))
    out_shape = jax.ShapeDtypeStruct((k // 2, n), jnp.uint32)
    return pl.pallas_call(
        _pack_pair_body,
        grid=(k // PACK_ROWS,),
        in_specs=[in_spec, in_spec],
        out_specs=[out_spec, out_spec],
        out_shape=[out_shape, out_shape],
        compiler_params=pltpu.CompilerParams(dimension_semantics=("arbitrary",),
                                             vmem_limit_bytes=VMEM_LIMIT_BYTES),
        name="pack_weight_pair",
    )(a_stack, b_stack)


def _weight(w_ref, cols=slice(None)):
    return pltpu.bitcast(w_ref[:, cols], BF16)


def _carry_halo(scr_ref, halo, tile, first_tile):
    @pl.when(first_tile)
    def _():
        scr_ref[:, 0:halo, :] = jnp.zeros((scr_ref.shape[0], halo, LANES), scr_ref.dtype)

    @pl.when(jnp.logical_not(first_tile))
    def _():
        scr_ref[:, 0:halo, :] = scr_ref[:, tile:tile + halo, :]


AB_SUB_TILE = 256


def _ab_body(x_ref, npre_ref, w_in_ref, wgrp_ref, ascale_ref, lng_ref, lnb_ref, ws_ref, bs_ref,
             w_out_ref, npost_ref, o_ref, a_scr, mix_scr, *, tile, d_a, d_b):
    i = pl.program_id(1)
    g_a = d_a // len(POOL_WINDOWS)
    dh_b = d_b // SGU_HEADS

    _carry_halo(a_scr, POOL_HALO, tile, i == 0)
    row = lax.broadcasted_iota(jnp.int32, (SGU_CHUNK, SGU_CHUNK), 0)
    col = lax.broadcasted_iota(jnp.int32, (SGU_CHUNK, SGU_CHUNK), 1)
    for t0 in range(0, tile, AB_SUB_TILE):
        trows = slice(t0, t0 + AB_SUB_TILE)
        xf = x_ref[trows, :]
        h = _rms_norm(xf, npre_ref[...]).astype(BF16)

        z_v = jax.nn.gelu(_dot(h, _weight(w_in_ref, slice(d_a + d_b, d_a + 2 * d_b))))
        v = _layer_norm(z_v, lng_ref[...], lnb_ref[...]).astype(BF16)

        a = _dot(h, _weight(w_in_ref, slice(0, d_a)))
        lo = POOL_HALO + t0
        for s in range(d_a // LANES):
            a_scr[s, lo:lo + AB_SUB_TILE, :] = a[:, s * LANES:(s + 1) * LANES]
        pos = i * tile + t0 + lax.broadcasted_iota(jnp.int32, (AB_SUB_TILE, LANES), 0)
        for g, w in enumerate(POOL_WINDOWS):
            cols = slice(g * g_a, (g + 1) * g_a)
            count = jnp.minimum(pos + 1, w).astype(F32)
            d = []
            for s in range(g * g_a // LANES, (g + 1) * g_a // LANES):
                ag = a_scr[s, lo:lo + AB_SUB_TILE, :]
                win = ag
                for k in range(1, w):
                    win = win + a_scr[s, lo - k:lo - k + AB_SUB_TILE, :]
                d.append((win / count - ag).astype(BF16))
            og = _dot(jnp.concatenate(d, axis=-1), wgrp_ref[g])
            mix_scr[trows, cols] = (og * ascale_ref[:, cols]).astype(BF16)

        z_u = jax.nn.gelu(_dot(h, _weight(w_in_ref, slice(d_a, d_a + d_b))))
        for hd in range(SGU_HEADS):
            w_tril = jnp.where(col <= row, ws_ref[hd], 0.0).astype(BF16)
            bias = bs_ref[:, hd:hd + 1]
            hcols = slice(hd * dh_b, (hd + 1) * dh_b)
            for c in range(AB_SUB_TILE // SGU_CHUNK):
                rows = slice(c * SGU_CHUNK, (c + 1) * SGU_CHUNK)
                mixed = _dot(w_tril, v[rows, hcols]) + bias
                mix_scr[t0 + c * SGU_CHUNK:t0 + (c + 1) * SGU_CHUNK, d_a + hd * dh_b:d_a + (hd + 1) * dh_b] = (
                    z_u[rows, hcols] * mixed).astype(BF16)

        y = _dot(mix_scr[trows, :], _weight(w_out_ref))
        o_ref[trows, :] = xf + _rms_norm(y, npost_ref[...])


def _single_buffered(shape):
    return pl.BlockSpec(shape, lambda *_: (0,) * len(shape), pipeline_mode=pl.Buffered(1))


def _ab_layer(x, npre, w_in, wgrp, ascale, lng, lnb, ws, bs, w_out, npost, *, tile):
    bsz, seq, d = x.shape
    d_a = ascale.shape[-1]
    d_b = lng.shape[-1]
    assert seq % tile == 0 and tile % AB_SUB_TILE == 0 and AB_SUB_TILE % SGU_CHUNK == 0
    assert (d_a // len(POOL_WINDOWS)) % LANES == 0 and AB_SUB_TILE >= POOL_HALO
    act_spec = pl.BlockSpec((None, tile, d), lambda b, i: (b, i, 0))
    operands = (npre.reshape(1, d), w_in, wgrp.astype(BF16), ascale.reshape(1, d_a),
                lng.reshape(1, d_b), lnb.reshape(1, d_b), ws, bs.T, w_out, npost.reshape(1, d))
    return pl.pallas_call(
        functools.partial(_ab_body, tile=tile, d_a=d_a, d_b=d_b),
        grid=(bsz, seq // tile),
        in_specs=[act_spec] + [_single_buffered(op.shape) for op in operands],
        out_specs=act_spec,
        out_shape=jax.ShapeDtypeStruct(x.shape, F32),
        scratch_shapes=[pltpu.VMEM((d_a // LANES, tile + POOL_HALO, LANES), F32),
                        pltpu.VMEM((tile, d_a + d_b), BF16)],
        compiler_params=pltpu.CompilerParams(
            dimension_semantics=("arbitrary", "arbitrary"), vmem_limit_bytes=VMEM_LIMIT_BYTES),
        name="ab_layer",
    )(x, *operands)


CONV_ROW_BLOCK = 128


def _conv_body(x_ref, npre_ref, w1_ref, b1_ref, wdw_ref, bdw_ref, lng_ref, lnb_ref, w2_ref, b2_ref,
               npost_ref, o_ref, g_scr, c_scr, *, tile, d_c):
    i = pl.program_id(1)
    xf = x_ref[...]
    h = _rms_norm(xf, npre_ref[...]).astype(BF16)

    _carry_halo(g_scr, CONV_HALO, tile, i == 0)
    base = CONV_HALO - (CONV_K - 1)
    for cb in range(d_c // MXU_COLS):
        lin_cols = slice(cb * MXU_COLS, (cb + 1) * MXU_COLS)
        gate_cols = slice(d_c + cb * MXU_COLS, d_c + (cb + 1) * MXU_COLS)
        lin = _dot(h, _weight(w1_ref, lin_cols)) + b1_ref[:, lin_cols]
        gate = _dot(h, _weight(w1_ref, gate_cols)) + b1_ref[:, gate_cols]
        glu = lin * jax.nn.sigmoid(gate)
        for half in range(MXU_COLS // LANES):
            s = cb * (MXU_COLS // LANES) + half
            cols = slice(s * LANES, (s + 1) * LANES)
            g_scr[s, CONV_HALO:CONV_HALO + tile, :] = glu[:, half * LANES:(half + 1) * LANES]
            for r0 in range(0, tile, CONV_ROW_BLOCK):
                acc = wdw_ref[0:1, cols] * g_scr[s, base + r0:base + r0 + CONV_ROW_BLOCK, :]
                for k in range(1, CONV_K):
                    acc = acc + wdw_ref[k:k + 1, cols] * g_scr[s, base + k + r0:base + k + r0 + CONV_ROW_BLOCK, :]
                c_scr[r0:r0 + CONV_ROW_BLOCK, cols] = acc + bdw_ref[:, cols]

    c = jax.nn.silu(_layer_norm(c_scr[...], lng_ref[...], lnb_ref[...])).astype(BF16)
    y = _dot(c, _weight(w2_ref)) + b2_ref[...]
    o_ref[...] = xf + _rms_norm(y, npost_ref[...])


def _conv_layer(x, npre, w1, b1, wdw, bdw, lng, lnb, w2, b2, npost, *, tile):
    bsz, seq, d = x.shape
    d_c = wdw.shape[-1]
    assert seq % tile == 0 and tile % CONV_ROW_BLOCK == 0 and tile >= CONV_HALO
    assert d_c % MXU_COLS == 0 and wdw.shape[0] == CONV_K
    act_spec = pl.BlockSpec((None, tile, d), lambda b, i: (b, i, 0))
    operands = (npre.reshape(1, d), w1, b1.reshape(1, 2 * d_c), wdw, bdw.reshape(1, d_c),
                lng.reshape(1, d_c), lnb.reshape(1, d_c), w2, b2.reshape(1, d),
                npost.reshape(1, d))
    return pl.pallas_call(
        functools.partial(_conv_body, tile=tile, d_c=d_c),
        grid=(bsz, seq // tile),
        in_specs=[act_spec] + [_single_buffered(op.shape) for op in operands],
        out_specs=act_spec,
        out_shape=jax.ShapeDtypeStruct(x.shape, F32),
        scratch_shapes=[pltpu.VMEM((d_c // LANES, tile + CONV_HALO, LANES), F32),
                        pltpu.VMEM((tile, d_c), F32)],
        compiler_params=pltpu.CompilerParams(
            dimension_semantics=("arbitrary", "arbitrary"), vmem_limit_bytes=VMEM_LIMIT_BYTES),
        name="conv_layer",
    )(x, *operands)


FFN_ROW_CHUNK = 256


def _ffn_body(x_ref, npre_ref, wg_ref, wu_ref, wd_ref, npost_ref, o_ref, h_scr):
    j = pl.program_id(1)
    last = pl.num_programs(1) - 1
    tile = o_ref.shape[0]
    ff_tile = wg_ref.shape[1]

    def swiglu(rows, col_block):
        h = h_scr[rows, :]
        out = None
        for c0 in range(0, wg_ref.shape[1], col_block):
            cols = slice(c0, c0 + col_block)
            a = jax.nn.silu(_dot(h, _weight(wg_ref, cols))) * _dot(h, _weight(wu_ref, cols))
            part = _dot(a.astype(BF16), pltpu.bitcast(wd_ref[c0 // 2:(c0 + col_block) // 2, :], BF16))
            out = part if out is None else out + part
        return out

    @pl.when(j == 0)
    def _():
        for r0 in range(0, tile, FFN_ROW_CHUNK):
            rows = slice(r0, r0 + FFN_ROW_CHUNK)
            h_scr[rows, :] = _rms_norm(x_ref[rows, :], npre_ref[...]).astype(BF16)
            o_ref[rows, :] = swiglu(rows, ff_tile)

    @pl.when(jnp.logical_and(j > 0, j < last))
    def _():
        o_ref[...] += swiglu(slice(None), MXU_COLS)

    @pl.when(j == last)
    def _():
        for r0 in range(0, tile, FFN_ROW_CHUNK):
            rows = slice(r0, r0 + FFN_ROW_CHUNK)
            y = o_ref[rows, :] + swiglu(rows, ff_tile)
            o_ref[rows, :] = x_ref[rows, :] + _rms_norm(y, npost_ref[...])


def _ffn_layer(x, npre, wg, wu, wd, npost, *, tile, ff_tile):
    bsz, seq, d = x.shape
    d_ff = wg.shape[-1]
    tokens = bsz * seq
    assert tokens % tile == 0 and tile % FFN_ROW_CHUNK == 0
    assert d_ff % ff_tile == 0 and d_ff // ff_tile >= 2
    act_spec = pl.BlockSpec((tile, d), lambda i, j: (i, 0))
    vec_spec = pl.BlockSpec((1, d), lambda i, j: (0, 0))
    out = pl.pallas_call(
        _ffn_body,
        grid=(tokens // tile, d_ff // ff_tile),
        in_specs=[act_spec, vec_spec,
                  pl.BlockSpec((d // 2, ff_tile), lambda i, j: (0, j)),
                  pl.BlockSpec((d // 2, ff_tile), lambda i, j: (0, j)),
                  pl.BlockSpec((ff_tile // 2, d), lambda i, j: (j, 0)),
                  vec_spec],
        out_specs=act_spec,
        out_shape=jax.ShapeDtypeStruct((tokens, d), F32),
        scratch_shapes=[pltpu.VMEM((tile, d), BF16)],
        compiler_params=pltpu.CompilerParams(
            dimension_semantics=("arbitrary", "arbitrary"), vmem_limit_bytes=VMEM_LIMIT_BYTES),
        name="ffn_layer",
    )(x.reshape(tokens, d), npre.reshape(1, d), wg, wu, wd, npost.reshape(1, d))
    return out.reshape(bsz, seq, d)


AB_TILE = 512
CONV_TILE = 256
FFN_TILE = 1024
FFN_FF_TILE = 512


def kernel(x, ab_norm_pre, ab_w_in, a_w_group, a_scale, b_ln_g, b_ln_b, b_w_s, b_b_s, ab_w_out, ab_norm_post, c_norm_pre, c_w_pw1, c_b_pw1, c_w_dw, c_b_dw, c_ln_g, c_ln_b, c_w_pw2, c_b_pw2, c_norm_post, f_norm_pre, f_w_gate, f_w_up, f_w_down, f_norm_post):
    depth = f_norm_pre.shape[0]
    for layer in range(depth):
        i = layer // 2
        if layer % 2 == 0:
            x = _ab_layer(x, ab_norm_pre[i], _pack_weight(ab_w_in, i), a_w_group[i], a_scale[i], b_ln_g[i],
                          b_ln_b[i], b_w_s[i], b_b_s[i], _pack_weight(ab_w_out, i), ab_norm_post[i],
                          tile=AB_TILE)
        else:
            x = _conv_layer(x, c_norm_pre[i], _pack_weight(c_w_pw1, i), c_b_pw1[i], c_w_dw[i], c_b_dw[i],
                            c_ln_g[i], c_ln_b[i], _pack_weight(c_w_pw2, i), c_b_pw2[i], c_norm_post[i],
                            tile=CONV_TILE)
        w_gate, w_up = _pack_weight_pair(f_w_gate, f_w_up, layer)
        x = _ffn_layer(x, f_norm_pre[layer], w_gate, w_up, _pack_weight(f_w_down, layer), f_norm_post[layer],
                       tile=FFN_TILE, ff_tile=FFN_FF_TILE)
    return x
```
